```python
import jax, jax.numpy as jnp
from jax import lax
import numpy as np

D_MODEL = 1024
BATCH = 2
SEQ = 8192
DEPTH = 1

N_META = 16
CONF_DIM = D_MODEL
CONF_KERNEL = 31
SSM_EXPAND = 2
D_INNER = SSM_EXPAND * D_MODEL
SSM_HEADDIM = 64
SSM_HEADS = D_INNER // SSM_HEADDIM
SSM_GROUPS = 4
HEADS_PER_GROUP = SSM_HEADS // SSM_GROUPS
D_STATE = 128
SSM_CONV = 7
CHUNK = 128
META_PAD = CHUNK - N_META
XBC_DIM = D_INNER + 2 * SSM_GROUPS * D_STATE
D_FF = -(-8 * D_MODEL // (3 * 256)) * 256
IN_SPLITS = (CONF_DIM, CONF_DIM, D_INNER, D_INNER, SSM_GROUPS * D_STATE, SSM_GROUPS * D_STATE,
             SSM_HEADS, SSM_HEADS, D_MODEL, D_MODEL)
IN_DIM = sum(IN_SPLITS)
EPS = 1e-6

kernel_name = "gated_conformer_bissd_hybrid_block"


def rmsnorm(x, g):
    xf = x.astype(jnp.float32)
    y = xf * lax.rsqrt(jnp.mean(xf * xf, axis=-1, keepdims=True) + EPS)
    return (y * g.astype(jnp.float32)).astype(x.dtype)


def layernorm(x, g, b):
    xf = x.astype(jnp.float32)
    mu = jnp.mean(xf, axis=-1, keepdims=True)
    var = jnp.mean(jnp.square(xf - mu), axis=-1, keepdims=True)
    y = (xf - mu) * lax.rsqrt(var + EPS)
    return (y * g.astype(jnp.float32) + b.astype(jnp.float32)).astype(x.dtype)


def gated_rmsnorm(y, z, w):
    shp = y.shape
    v = y.astype(jnp.float32) * jax.nn.silu(z.astype(jnp.float32))
    v = v.reshape(shp[:-1] + (SSM_GROUPS, shp[-1] // SSM_GROUPS))
    v = v * lax.rsqrt(jnp.mean(v * v, axis=-1, keepdims=True) + EPS)
    return (v.reshape(shp) * w.astype(jnp.float32)).astype(y.dtype)


def dwconv(x, w, b):
    k, c = w.shape
    half = (k - 1) // 2
    y = lax.conv_general_dilated(x, w[:, None, :].astype(x.dtype), window_strides=(1,),
                                 padding=[(half, half)], dimension_numbers=('NWC', 'WIO', 'NWC'),
                                 feature_group_count=c)
    return y + b


def segsum(v):
    t = v.shape[-1]
    vv = jnp.broadcast_to(v[..., :, None], v.shape + (t,))
    vv = jnp.where(jnp.tril(jnp.ones((t, t), dtype=bool), -1), vv, 0.0)
    s = jnp.cumsum(vv, axis=-2)
    return jnp.where(jnp.tril(jnp.ones((t, t), dtype=bool), 0), s, -jnp.inf)


def ssd_chunked(xh, dt, a, bm, cm):
    b, t, g, e, p = xh.shape
    n = bm.shape[-1]
    nc = t // CHUNK
    xdt = (xh * dt[..., None]).reshape(b, nc, CHUNK, g, e, p)
    adt = jnp.moveaxis((dt * a).reshape(b, nc, CHUNK, g, e), 2, -1)
    bc = bm.reshape(b, nc, CHUNK, g, n)
    cc = cm.reshape(b, nc, CHUNK, g, n)
    a_cum = jnp.cumsum(adt, axis=-1)
    lmat = jnp.exp(segsum(adt))
    cb = jnp.einsum('bclgn,bcsgn->bcgls', cc, bc)
    y_diag = jnp.einsum('bcgels,bcsgep->bclgep', cb[:, :, :, None] * lmat, xdt)
    decay_states = jnp.exp(a_cum[..., -1:] - a_cum)
    states = jnp.einsum('bclgn,bcgel,bclgep->bcgepn', bc, decay_states, xdt)
    tot = jnp.pad(a_cum[..., -1], ((0, 0), (1, 0), (0, 0), (0, 0)))
    decay_chunk = jnp.exp(segsum(jnp.moveaxis(tot, 1, -1)))
    states = jnp.concatenate([jnp.zeros_like(states[:, :1]), states], axis=1)
    states_in = jnp.einsum('bgezc,bcgepn->bzgepn', decay_chunk, states)[:, :-1]
    y_off = jnp.einsum('bclgn,bcgepn,bcgel->bclgep', cc, states_in, jnp.exp(a_cum))
    return (y_diag + y_off).reshape(b, t, g, e, p)


def pad_front(v):
    return jnp.pad(v, ((0, 0), (META_PAD, 0)) + ((0, 0),) * (v.ndim - 2))


def mixer_block(h, w_in, conv_dw_w, conv_dw_b, conv_ln_g, conv_ln_b, conv_out_w,
                ssm_conv_w, ssm_conv_b, dt_bias_f, dt_bias_b, a_log_f, a_log_b,
                ssm_d, ssm_norm_w, ssm_out_w, w_o):
    bsz, length, _ = h.shape
    offs = [int(o) for o in np.cumsum(IN_SPLITS)[:-1]]
    u = h @ w_in
    g_val, g_gate, z, xs, bs, cs, dtf, dtb, gate_conv, gate_ssm = jnp.split(u, offs, axis=-1)

    a = g_val * jax.nn.sigmoid(g_gate)
    a = dwconv(a, conv_dw_w, conv_dw_b)
    a = jax.nn.silu(layernorm(a, conv_ln_g, conv_ln_b))
    y_conv = a @ conv_out_w

    xbc = jax.nn.silu(dwconv(jnp.concatenate([xs, bs, cs], axis=-1), ssm_conv_w, ssm_conv_b))
    xs, bs, cs = jnp.split(xbc, [D_INNER, D_INNER + SSM_GROUPS * D_STATE], axis=-1)
    xh = xs.astype(jnp.float32).reshape(bsz, length, SSM_GROUPS, HEADS_PER_GROUP, SSM_HEADDIM)
    bm = bs.astype(jnp.float32).reshape(bsz, length, SSM_GROUPS, D_STATE)
    cm = cs.astype(jnp.float32).reshape(bsz, length, SSM_GROUPS, D_STATE)
    shp_dt = (bsz, length, SSM_GROUPS, HEADS_PER_GROUP)
    dt_f = jax.nn.softplus((dtf + dt_bias_f).astype(jnp.float32)).reshape(shp_dt)
    dt_b = jax.nn.softplus((dtb + dt_bias_b).astype(jnp.float32)).reshape(shp_dt)
    a_f = -jnp.exp(a_log_f.astype(jnp.float32)).reshape(SSM_GROUPS, HEADS_PER_GROUP)
    a_b = -jnp.exp(a_log_b.astype(jnp.float32)).reshape(SSM_GROUPS, HEADS_PER_GROUP)
    xp, bp, cp = pad_front(xh), pad_front(bm), pad_front(cm)
    dfp, dbp = pad_front(dt_f), pad_front(dt_b)
    rev = lambda v: jnp.flip(v, axis=1)
    y_fwd = ssd_chunked(xp, dfp, a_f, bp, cp)
    y_bwd = rev(ssd_chunked(rev(xp), rev(dbp), a_b, rev(bp), rev(cp)))
    d_skip = ssm_d.astype(jnp.float32).reshape(SSM_GROUPS, HEADS_PER_GROUP)[..., None]
    y = (y_fwd + y_bwd)[:, META_PAD:] + d_skip * xh
    y = y.reshape(bsz, length, D_INNER).astype(h.dtype)
    y_ssm = gated_rmsnorm(y, z, ssm_norm_w) @ ssm_out_w

    merged = jax.nn.sigmoid(gate_conv) * y_conv + jax.nn.sigmoid(gate_ssm) * y_ssm
    return merged @ w_o


def swiglu(h, w_gate, w_up, w_down):
    return (jax.nn.silu(h @ w_gate) * (h @ w_up)) @ w_down


def setup_inputs(seed: int = 0) -> dict:
    key = jax.random.key(seed)
    ks = jax.random.split(key, 24)
    f32 = jnp.float32
    nrm = lambda k, shp, s: jax.random.normal(k, shp, f32) * s
    dt0 = jnp.exp(jax.random.uniform(ks[12], (DEPTH, SSM_HEADS), f32) * (np.log(0.1) - np.log(0.001)) + np.log(0.001))
    dt1 = jnp.exp(jax.random.uniform(ks[13], (DEPTH, SSM_HEADS), f32) * (np.log(0.1) - np.log(0.001)) + np.log(0.001))
    inv_softplus = lambda d: d + jnp.log(-jnp.expm1(-d))
    return {
        "x": nrm(ks[0], (BATCH, SEQ, D_MODEL), 1.0),
        "meta_tokens": nrm(ks[1], (N_META, D_MODEL), 1.0),
        "norm_mix": 1.0 + nrm(ks[2], (DEPTH, D_MODEL), 0.02),
        "w_in": nrm(ks[3], (DEPTH, D_MODEL, IN_DIM), D_MODEL ** -0.5),
        "conv_dw_w": nrm(ks[4], (DEPTH, CONF_KERNEL, CONF_DIM), CONF_KERNEL ** -0.5),
        "conv_dw_b": nrm(ks[5], (DEPTH, CONF_DIM), 0.02),
        "conv_ln_g": 1.0 + nrm(ks[6], (DEPTH, CONF_DIM), 0.02),
        "conv_ln_b": nrm(ks[7], (DEPTH, CONF_DIM), 0.02),
        "conv_out_w": nrm(ks[8], (DEPTH, CONF_DIM, D_MODEL), CONF_DIM ** -0.5),
        "ssm_conv_w": nrm(ks[9], (DEPTH, SSM_CONV, XBC_DIM), SSM_CONV ** -0.5),
        "ssm_conv_b": nrm(ks[10], (DEPTH, XBC_DIM), 0.02),
        "dt_bias_f": inv_softplus(dt0),
        "dt_bias_b": inv_softplus(dt1),
        "a_log_f": jnp.log(jax.random.uniform(ks[14], (DEPTH, SSM_HEADS), f32, 1.0, 16.0)),
        "a_log_b": jnp.log(jax.random.uniform(ks[15], (DEPTH, SSM_HEADS), f32, 1.0, 16.0)),
        "ssm_d": 1.0 + nrm(ks[16], (DEPTH, SSM_HEADS), 0.02),
        "ssm_norm_w": 1.0 + nrm(ks[17], (DEPTH, D_INNER), 0.02),
        "ssm_out_w": nrm(ks[18], (DEPTH, D_INNER, D_MODEL), D_INNER ** -0.5),
        "w_o": nrm(ks[19], (DEPTH, D_MODEL, D_MODEL), D_MODEL ** -0.5),
        "norm_ffn": 1.0 + nrm(ks[20], (DEPTH, D_MODEL), 0.02),
        "w_gate": nrm(ks[21], (DEPTH, D_MODEL, D_FF), D_MODEL ** -0.5),
        "w_up": nrm(ks[22], (DEPTH, D_MODEL, D_FF), D_MODEL ** -0.5),
        "w_down": nrm(ks[23], (DEPTH, D_FF, D_MODEL), D_FF ** -0.5),
        "norm_final": 1.0 + nrm(ks[11], (D_MODEL,), 0.02),
    }


def reference(x, meta_tokens, norm_mix, w_in, conv_dw_w, conv_dw_b, conv_ln_g, conv_ln_b,
              conv_out_w, ssm_conv_w, ssm_conv_b, dt_bias_f, dt_bias_b, a_log_f, a_log_b,
              ssm_d, ssm_norm_w, ssm_out_w, w_o, norm_ffn, w_gate, w_up, w_down, norm_final):
    bsz = x.shape[0]
    meta = jnp.broadcast_to(meta_tokens[None].astype(x.dtype), (bsz, N_META, x.shape[-1]))
    hs = jnp.concatenate([meta, x], axis=1)
    for i in range(DEPTH):
        hn = rmsnorm(hs, norm_mix[i])
        hs = hs + mixer_block(hn, w_in[i], conv_dw_w[i], conv_dw_b[i], conv_ln_g[i], conv_ln_b[i],
                              conv_out_w[i], ssm_conv_w[i], ssm_conv_b[i], dt_bias_f[i], dt_bias_b[i],
                              a_log_f[i], a_log_b[i], ssm_d[i], ssm_norm_w[i], ssm_out_w[i], w_o[i])
        hs = hs + swiglu(rmsnorm(hs, norm_ffn[i]), w_gate[i], w_up[i], w_down[i])
    return rmsnorm(hs, norm_final)[:, N_META:]
```

```python
import functools

import jax
import jax.numpy as jnp
from jax import lax
from jax.experimental import pallas as pl
from jax.experimental.pallas import tpu as pltpu

F32 = jnp.float32
BF16 = jnp.bfloat16

N_META = 16
CHUNK = 128
META_PAD = CHUNK - N_META
CONF_KERNEL = 31
SSM_CONV = 7
SSM_HEADDIM = 64
SSM_GROUPS = 4
D_STATE = 128
EPS = 1e-6
HALO = 16
VMEM_LIMIT = 56 * 1024 * 1024
NEG_BIG = -1e30


def _sigmoid(v):
    return 1.0 / (1.0 + jnp.exp(-v))


def _silu(v):
    return v * _sigmoid(v)


def _softplus(v):
    return jnp.maximum(v, 0.0) + jnp.log1p(jnp.exp(-jnp.abs(v)))


def _dot(a, b):
    return jnp.dot(a, b, preferred_element_type=F32)


def _dot_exact(a, b):
    return jnp.dot(a, b, preferred_element_type=F32, precision=lax.Precision.HIGHEST)


def _resident(shape):
    nd = len(shape)
    return pl.BlockSpec(shape, lambda *_: (0,) * nd, pipeline_mode=pl.Buffered(1))


def _inproj_kernel(hs_ref, g_ref, w_glu_ref, w_z_ref, w_xbc_ref, w_dt_ref, w_dtt_ref, w_gates_ref,
                   dtb_ref, dtbt_ref,
                   a_ref, z_ref, xbc_ref, dt_ref, dtt_ref, gates_ref, *, tm, chunks_per_frame, conf):
    x = hs_ref[...]
    ms = jnp.mean(x * x, axis=-1, keepdims=True)
    hn = (x * lax.rsqrt(ms + EPS) * g_ref[...]).astype(BF16)

    glu = _dot(hn, w_glu_ref[...])
    a_ref[...] = (glu[:, :conf] * _sigmoid(glu[:, conf:])).astype(BF16)
    z_ref[...] = _dot(hn, w_z_ref[...]).astype(BF16)
    xbc_ref[...] = _dot(hn, w_xbc_ref[...]).astype(BF16)
    gates_ref[...] = _sigmoid(_dot(hn, w_gates_ref[...])).astype(BF16)

    dt = _softplus(_dot(hn, w_dt_ref[...]) + dtb_ref[...])
    dtt = _softplus(lax.dot_general(w_dtt_ref[...], hn, (((1,), (1,)), ((), ())),
                                    preferred_element_type=F32) + dtbt_ref[...])
    row = lax.broadcasted_iota(jnp.int32, (CHUNK, 1), 0)
    col = lax.broadcasted_iota(jnp.int32, (1, CHUNK), 1)
    for k in range(tm // CHUNK):
        chunk = (pl.program_id(0) * (tm // CHUNK) + k) % chunks_per_frame
        first_live = jnp.where(chunk == chunks_per_frame - 1, META_PAD, 0)
        sl = slice(k * CHUNK, (k + 1) * CHUNK)
        dt_ref[sl, :] = jnp.where(row >= first_live, dt[sl, :], 0.0)
        dtt_ref[:, sl] = jnp.where(col >= first_live, dtt[:, sl], 0.0)


def _inproj(hs, g, w_glu, w_z, w_xbc, w_dt, w_dtt, w_gates, dtb, dtbt, *, tm, chunks_per_frame):
    rows, d = hs.shape
    conf = w_glu.shape[1] // 2
    n_z, n_xbc, n_dt, n_g = w_z.shape[1], w_xbc.shape[1], w_dt.shape[1], w_gates.shape[1]
    row_spec = lambda n: pl.BlockSpec((tm, n), lambda i: (i, 0))
    return pl.pallas_call(
        functools.partial(_inproj_kernel, tm=tm, chunks_per_frame=chunks_per_frame, conf=conf),
        grid=(rows // tm,),
        in_specs=[row_spec(d), _resident(g.shape), _resident(w_glu.shape), _resident(w_z.shape),
                  _resident(w_xbc.shape), _resident(w_dt.shape), _resident(w_dtt.shape),
                  _resident(w_gates.shape), _resident(dtb.shape), _resident(dtbt.shape)],
        out_specs=[row_spec(conf), row_spec(n_z), row_spec(n_xbc), row_spec(n_dt),
                   pl.BlockSpec((n_dt, tm), lambda i: (0, i)), row_spec(n_g)],
        out_shape=[jax.ShapeDtypeStruct((rows, conf), BF16), jax.ShapeDtypeStruct((rows, n_z), BF16),
                   jax.ShapeDtypeStruct((rows, n_xbc), BF16), jax.ShapeDtypeStruct((rows, n_dt), F32),
                   jax.ShapeDtypeStruct((n_dt, rows), F32), jax.ShapeDtypeStruct((rows, n_g), BF16)],
        compiler_params=pltpu.CompilerParams(dimension_semantics=("parallel",),
                                             vmem_limit_bytes=VMEM_LIMIT),
        name="inproj",
    )(hs, g, w_glu, w_z, w_xbc, w_dt, w_dtt, w_gates, dtb, dtbt)


def _fill_window(buf_ref, prev_ref, main_ref, next_ref):
    buf_ref[0:HALO, :] = prev_ref[0].astype(F32)
    buf_ref[HALO:HALO + CHUNK, :] = main_ref[0].astype(F32)
    buf_ref[HALO + CHUNK:, :] = next_ref[0].astype(F32)


def _dwconv_cols(buf_ref, w_ref, b_ref, ksize, c0, c1):
    half = (ksize - 1) // 2
    acc = jnp.broadcast_to(b_ref[:, c0:c1], (CHUNK, c1 - c0))
    for k in range(ksize):
        start = HALO - half + k
        acc = acc + buf_ref[start:start + CHUNK, c0:c1] * w_ref[k:k + 1, c0:c1]
    return acc


def _conv_a_kernel(prev_ref, main_ref, next_ref, gate_ref, w_ref, b_ref, lng_ref, lnb_ref, wout_ref,
                   o_ref, buf_ref, conv_ref, *, col_block):
    _fill_window(buf_ref, prev_ref, main_ref, next_ref)
    c = main_ref.shape[2]
    for c0 in range(0, c, col_block):
        conv_ref[:, c0:c0 + col_block] = _dwconv_cols(buf_ref, w_ref, b_ref, CONF_KERNEL, c0, c0 + col_block)
    v = conv_ref[...]
    mu = jnp.mean(v, axis=-1, keepdims=True)
    vc = v - mu
    var = jnp.mean(vc * vc, axis=-1, keepdims=True)
    h = _silu(vc * lax.rsqrt(var + EPS) * lng_ref[...] + lnb_ref[...]).astype(BF16)
    y = _dot(h, wout_ref[...])
    o_ref[0] = (gate_ref[0].astype(F32) * y).astype(BF16)


def _halo_maps(n_x_chunks):
    per = CHUNK // HALO
    meta_rows_block = (n_x_chunks * CHUNK + META_PAD) // HALO
    prev_map = lambda b, j: (b, jnp.where(j == 0, meta_rows_block, j * per - 1), 0)
    next_map = lambda b, j: (b, jnp.where(j == n_x_chunks, 0, j * per + per), 0)
    return prev_map, next_map


def _conv_a(a, gates, w, b, lng, lnb, wout, *, n_x_chunks):
    bsz, _, c = a.shape
    prev_map, next_map = _halo_maps(n_x_chunks)
    return pl.pallas_call(
        functools.partial(_conv_a_kernel, col_block=256),
        grid=(bsz, n_x_chunks),
        in_specs=[pl.BlockSpec((1, HALO, c), prev_map),
                  pl.BlockSpec((1, CHUNK, c), lambda b, j: (b, j, 0)),
                  pl.BlockSpec((1, HALO, c), next_map),
                  pl.BlockSpec((1, CHUNK, c), lambda b, j: (b, j, 0)),
                  _resident(w.shape), _resident(b.shape), _resident(lng.shape), _resident(lnb.shape),
                  _resident(wout.shape)],
        out_specs=pl.BlockSpec((1, CHUNK, c), lambda b, j: (b, j, 0)),
        out_shape=jax.ShapeDtypeStruct((bsz, n_x_chunks * CHUNK, c), BF16),
        scratch_shapes=[pltpu.VMEM((CHUNK + 2 * HALO, c), F32), pltpu.VMEM((CHUNK, c), F32)],
        compiler_params=pltpu.CompilerParams(dimension_semantics=("parallel", "parallel"),
                                             vmem_limit_bytes=VMEM_LIMIT),
        name="conv_a",
    )(a, a, a, gates, w, b, lng, lnb, wout)


def _conv_s_kernel(prev_ref, main_ref, next_ref, w_ref, b_ref, o_ref, buf_ref, *, col_block):
    _fill_window(buf_ref, prev_ref, main_ref, next_ref)
    c = main_ref.shape[2]
    for c0 in range(0, c, col_block):
        o_ref[0, :, c0:c0 + col_block] = _silu(
            _dwconv_cols(buf_ref, w_ref, b_ref, SSM_CONV, c0, c0 + col_block)).astype(BF16)


def _conv_s(xbc, w, b, *, n_x_chunks):
    bsz, frame, c = xbc.shape
    prev_map, next_map = _halo_maps(n_x_chunks)
    return pl.pallas_call(
        functools.partial(_conv_s_kernel, col_block=512),
        grid=(bsz, n_x_chunks + 1),
        in_specs=[pl.BlockSpec((1, HALO, c), prev_map),
                  pl.BlockSpec((1, CHUNK, c), lambda b, j: (b, j, 0)),
                  pl.BlockSpec((1, HALO, c), next_map),
                  _resident(w.shape), _resident(b.shape)],
        out_specs=pl.BlockSpec((1, CHUNK, c), lambda b, j: (b, j, 0)),
        out_shape=jax.ShapeDtypeStruct((bsz, frame, c), BF16),
        scratch_shapes=[pltpu.VMEM((CHUNK + 2 * HALO, c), F32)],
        compiler_params=pltpu.CompilerParams(dimension_semantics=("parallel", "parallel"),
                                             vmem_limit_bytes=VMEM_LIMIT),
        name="conv_s",
    )(xbc, xbc, xbc, w, b)


def _spread_matrix(n):
    lane = lax.broadcasted_iota(jnp.int32, (n, n * SSM_HEADDIM), 1)
    start = lax.broadcasted_iota(jnp.int32, (n, n * SSM_HEADDIM), 0) * SSM_HEADDIM
    return jnp.where(jnp.logical_and(lane >= start, lane < start + SSM_HEADDIM), 1.0, 0.0)


def _ssd_direction(x_ref, dt_ref, dtt_ref, alog_row_ref, alog_col_ref, dskip_ref, y_ref, state_ref, yd_ref,
                   *, reverse, d_inner, hpg):
    gp = hpg * SSM_HEADDIM
    li = lax.broadcasted_iota(jnp.int32, (CHUNK, CHUNK), 0)
    si = lax.broadcasted_iota(jnp.int32, (CHUNK, CHUNK), 1)
    lower = (li >= si).astype(F32)
    upper = (li <= si).astype(F32)
    visible = (li <= si) if reverse else (li >= si)
    spread = _spread_matrix(hpg).astype(F32)
    spread2 = _spread_matrix(2 * hpg).astype(BF16)
    edge = 0 if reverse else CHUNK - 1
    hoff = dt_ref.shape[2] // 2 if reverse else 0

    for g in range(SSM_GROUPS):
        h0 = hoff + g * hpg
        xg = x_ref[0, :, g * gp:(g + 1) * gp]
        bm = x_ref[0, :, d_inner + g * D_STATE:d_inner + (g + 1) * D_STATE]
        cm = x_ref[0, :, d_inner + (SSM_GROUPS + g) * D_STATE:d_inner + (SSM_GROUPS + g + 1) * D_STATE]
        dt_c = dt_ref[0, :, h0:h0 + hpg]
        dt_r = dtt_ref[h0:h0 + hpg, :]
        adt_c = dt_c * -jnp.exp(alog_row_ref[:, h0:h0 + hpg])
        adt_r = dt_r * -jnp.exp(alog_col_ref[h0:h0 + hpg, :])
        if reverse:
            cum_c = _dot_exact(upper, adt_c)
            cum_r = _dot_exact(adt_r, lower)
        else:
            cum_c = _dot_exact(lower, adt_c)
            cum_r = _dot_exact(adt_r, upper)
        tot = cum_c[edge:edge + 1, :]
        to_edge = jnp.exp(tot - cum_c) * dt_c
        from_edge = jnp.exp(cum_c)

        cb = lax.dot_general(cm, bm, (((1,), (1,)), ((), ())), preferred_element_type=F32)
        for h in range(hpg):
            diff = (jnp.broadcast_to(cum_c[:, h:h + 1], (CHUNK, CHUNK))
                    - jnp.broadcast_to(cum_r[h:h + 1, :], (CHUNK, CHUNK)))
            decay = jnp.exp(jnp.where(visible, diff, NEG_BIG))
            m = (cb * decay * jnp.broadcast_to(dt_r[h:h + 1, :], (CHUNK, CHUNK))).astype(BF16)
            yd_ref[:, h * SSM_HEADDIM:(h + 1) * SSM_HEADDIM] = _dot(
                m, xg[:, h * SSM_HEADDIM:(h + 1) * SSM_HEADDIM])

        wide = _dot(jnp.concatenate([to_edge, from_edge], axis=1).astype(BF16), spread2)
        tot_wide = _dot_exact(jnp.broadcast_to(jnp.exp(tot), (8, hpg)), spread)[0:1, :]
        xf = xg.astype(F32)
        weighted = (xf * wide[:, :gp]).astype(BF16)
        s_in = state_ref[g]
        s_local = lax.dot_general(bm, weighted, (((0,), (0,)), ((), ())), preferred_element_type=F32)
        y_off = _dot(cm, s_in.astype(BF16)) * wide[:, gp:]
        state_ref[g] = tot_wide * s_in + s_local
        y = yd_ref[...] + y_off
        if not reverse:
            y = y + dskip_ref[:, g * gp:(g + 1) * gp] * xf
        y_ref[0, :, g * gp:(g + 1) * gp] = y.astype(y_ref.dtype)


def _ssd_kernel(xf_ref, dtf_ref, dttf_ref, xb_ref, dtb_ref, dttb_ref, alog_row_ref, alog_col_ref, dskip_ref,
                yf_ref, yb_ref, sf_ref, sb_ref, yd_ref, *, d_inner, hpg):
    @pl.when(pl.program_id(1) == 0)
    def _():
        sf_ref[...] = jnp.zeros_like(sf_ref)
        sb_ref[...] = jnp.zeros_like(sb_ref)

    _ssd_direction(xf_ref, dtf_ref, dttf_ref, alog_row_ref, alog_col_ref, dskip_ref, yf_ref, sf_ref, yd_ref,
                   reverse=False, d_inner=d_inner, hpg=hpg)
    _ssd_direction(xb_ref, dtb_ref, dttb_ref, alog_row_ref, alog_col_ref, dskip_ref, yb_ref, sb_ref, yd_ref,
                   reverse=True, d_inner=d_inner, hpg=hpg)


def _ssd(xbc, dt, dtt, alog_row, alog_col, dskip, *, n_x_chunks, d_inner):
    bsz, frame, c = xbc.shape
    n_heads2 = dt.shape[2]
    hpg = n_heads2 // 2 // SSM_GROUPS
    nc = n_x_chunks + 1
    fwd = lambda i: (i + n_x_chunks) % nc
    bwd = lambda i: jnp.where(i == n_x_chunks, n_x_chunks, n_x_chunks - 1 - i)
    row_block = lambda n, order: pl.BlockSpec((1, CHUNK, n), lambda b, i: (b, order(i), 0))
    col_block = lambda order: pl.BlockSpec((n_heads2, CHUNK), lambda b, i: (0, b * nc + order(i)))
    y_shape = jax.ShapeDtypeStruct((bsz, frame, d_inner), BF16)
    return pl.pallas_call(
        functools.partial(_ssd_kernel, d_inner=d_inner, hpg=hpg),
        grid=(bsz, nc),
        in_specs=[row_block(c, fwd), row_block(n_heads2, fwd), col_block(fwd),
                  row_block(c, bwd), row_block(n_heads2, bwd), col_block(bwd),
                  _resident(alog_row.shape), _resident(alog_col.shape), _resident(dskip.shape)],
        out_specs=[row_block(d_inner, fwd), row_block(d_inner, bwd)],
        out_shape=[y_shape, y_shape],
        scratch_shapes=[pltpu.VMEM((SSM_GROUPS, D_STATE, hpg * SSM_HEADDIM), F32),
                        pltpu.VMEM((SSM_GROUPS, D_STATE, hpg * SSM_HEADDIM), F32),
                        pltpu.VMEM((CHUNK, hpg * SSM_HEADDIM), F32)],
        compiler_params=pltpu.CompilerParams(dimension_semantics=("parallel", "arbitrary"),
                                             vmem_limit_bytes=VMEM_LIMIT),
        name="ssd",
    )(xbc, dt, dtt, xbc, dt, dtt, alog_row, alog_col, dskip)


def _tail_kernel(x_ref, yf_ref, yb_ref, z_ref, gate_ref, yc_ref, nw_ref, wso_ref, wo_ref, nffn_ref,
                 wg_ref, wu_ref, wd_ref, nfin_ref, o_ref):
    d_inner = yf_ref.shape[2]
    gc = d_inner // SSM_GROUPS
    v = (yf_ref[0].astype(F32) + yb_ref[0].astype(F32)) * _silu(z_ref[0].astype(F32))
    parts = []
    for g in range(SSM_GROUPS):
        vg = v[:, g * gc:(g + 1) * gc]
        ms = jnp.mean(vg * vg, axis=-1, keepdims=True)
        parts.append((vg * lax.rsqrt(ms + EPS) * nw_ref[:, g * gc:(g + 1) * gc]).astype(BF16))
    y_ssm = _dot(jnp.concatenate(parts, axis=1), wso_ref[...])
    merged = yc_ref[0].astype(F32) + gate_ref[0].astype(F32) * y_ssm
    hs = x_ref[0] + _dot(merged.astype(BF16), wo_ref[...])

    ms = jnp.mean(hs * hs, axis=-1, keepdims=True)
    hn = (hs * lax.rsqrt(ms + EPS) * nffn_ref[...]).astype(BF16)
    act = (_silu(_dot(hn, wg_ref[...])) * _dot(hn, wu_ref[...])).astype(BF16)
    hs = hs + _dot(act, wd_ref[...])

    ms = jnp.mean(hs * hs, axis=-1, keepdims=True)
    o_ref[0] = hs * lax.rsqrt(ms + EPS) * nfin_ref[...]


def _tail(x, yf, yb, z, gates, yc, nw, wso, wo, nffn, wg, wu, wd, nfin, *, tm):
    bsz, seq, d = x.shape
    d_inner = yf.shape[2]
    blk = lambda n, col=0: pl.BlockSpec((1, tm, n), lambda b, j: (b, j, col))
    return pl.pallas_call(
        _tail_kernel,
        grid=(bsz, seq // tm),
        in_specs=[blk(d), blk(d_inner), blk(d_inner), blk(d_inner), blk(d, 1), blk(d),
                  _resident(nw.shape), _resident(wso.shape), _resident(wo.shape), _resident(nffn.shape),
                  _resident(wg.shape), _resident(wu.shape), _resident(wd.shape), _resident(nfin.shape)],
        out_specs=blk(d),
        out_shape=jax.ShapeDtypeStruct((bsz, seq, d), F32),
        compiler_params=pltpu.CompilerParams(dimension_semantics=("parallel", "parallel"),
                                             vmem_limit_bytes=VMEM_LIMIT),
        name="tail",
    )(x, yf, yb, z, gates, yc, nw, wso, wo, nffn, wg, wu, wd, nfin)


def kernel(x, meta_tokens, norm_mix, w_in, conv_dw_w, conv_dw_b, conv_ln_g, conv_ln_b, conv_out_w,
           ssm_conv_w, ssm_conv_b, dt_bias_f, dt_bias_b, a_log_f, a_log_b, ssm_d, ssm_norm_w, ssm_out_w,
           w_o, norm_ffn, w_gate, w_up, w_down, norm_final):
    bsz, seq, d = x.shape
    assert norm_mix.shape[0] == 1 and seq % CHUNK == 0 and meta_tokens.shape[0] == N_META
    conf = conv_dw_w.shape[2]
    d_inner = ssm_norm_w.shape[1]
    n_heads = ssm_d.shape[1]
    gn = SSM_GROUPS * D_STATE
    n_x_chunks = seq // CHUNK
    frame = seq + CHUNK
    rows = bsz * frame

    hs = jnp.concatenate([x, jnp.zeros((bsz, META_PAD, d), x.dtype),
                          jnp.broadcast_to(meta_tokens[None].astype(x.dtype), (bsz, N_META, d))], axis=1)

    w = w_in[0]
    o_z = 2 * conf
    o_xbc = o_z + d_inner
    o_dt = o_xbc + d_inner + 2 * gn
    o_g = o_dt + 2 * n_heads
    w_dt = w[:, o_dt:o_g]
    dt_bias = jnp.concatenate([dt_bias_f[0], dt_bias_b[0]])
    a_log = jnp.concatenate([a_log_f[0], a_log_b[0]])

    a, z, xbc, dt, dtt, gates = _inproj(
        hs.reshape(rows, d), norm_mix, w[:, :o_z].astype(BF16), w[:, o_z:o_xbc].astype(BF16),
        w[:, o_xbc:o_dt].astype(BF16), w_dt.astype(BF16), w_dt.T.astype(BF16), w[:, o_g:].astype(BF16),
        dt_bias[None, :], dt_bias[:, None], tm=2 * CHUNK, chunks_per_frame=n_x_chunks + 1)
    a = a.reshape(bsz, frame, conf)
    z = z.reshape(bsz, frame, d_inner)
    xbc = xbc.reshape(bsz, frame, d_inner + 2 * gn)
    dt = dt.reshape(bsz, frame, 2 * n_heads)
    gates = gates.reshape(bsz, frame, 2 * d)

    yc = _conv_a(a, gates, conv_dw_w[0], conv_dw_b, conv_ln_g, conv_ln_b, conv_out_w[0].astype(BF16),
                 n_x_chunks=n_x_chunks)
    xbc = _conv_s(xbc, ssm_conv_w[0], ssm_conv_b, n_x_chunks=n_x_chunks)
    yf, yb = _ssd(xbc, dt, dtt, a_log[None, :], a_log[:, None],
                  jnp.repeat(ssm_d[0], SSM_HEADDIM)[None, :], n_x_chunks=n_x_chunks, d_inner=d_inner)
    return _tail(x, yf, yb, z, gates, yc, ssm_norm_w, ssm_out_w[0].astype(BF16), w_o[0].astype(BF16),
                 norm_ffn, w_gate[0].astype(BF16), w_up[0].astype(BF16), w_down[0].astype(BF16),
                 norm_final[None, :], tm=2 * CHUNK)
```

```python
import functools

import jax
import jax.numpy as jnp
import numpy as np
from jax import lax
from jax.experimental import pallas as pl
from jax.experimental.pallas import tpu as pltpu

F32 = jnp.float32
BF16 = jnp.bfloat16

N_META = 16
CHUNK = 128
META_PAD = CHUNK - N_META
CONF_KERNEL = 31
SSM_CONV = 7
SSM_HEADDIM = 64
SSM_GROUPS = 4
D_STATE = 128
EPS = 1e-6
LANES = 128
HALO = 16
VMEM_LIMIT = 56 * 1024 * 1024
NEG_BIG = -1e30


def _sigmoid(v):
    return 1.0 / (1.0 + jnp.exp(-v))


def _silu(v):
    return v * _sigmoid(v)


def _softplus(v):
    return jnp.maximum(v, 0.0) + jnp.log1p(jnp.exp(-jnp.abs(v)))


def _dot(a, b):
    return jnp.dot(a, b, preferred_element_type=F32)


def _dot_exact(a, b):
    return jnp.dot(a, b, preferred_element_type=F32, precision=lax.Precision.HIGHEST)


def _resident(shape):
    nd = len(shape)
    return pl.BlockSpec(shape, lambda *_: (0,) * nd, pipeline_mode=pl.Buffered(1))


def _inproj_kernel(hs_ref, g_ref, w_glu_ref, w_z_ref, w_xbc_ref, w_dt_ref, w_dtt_ref, w_gates_ref,
                   dtb_ref, dtbt_ref,
                   a_ref, z_ref, xbc_ref, dt_ref, dtt_ref, gates_ref, *, tm, chunks_per_frame, conf):
    x = hs_ref[...]
    ms = jnp.mean(x * x, axis=-1, keepdims=True)
    hn = (x * lax.rsqrt(ms + EPS) * g_ref[...]).astype(BF16)

    glu = _dot(hn, w_glu_ref[...])
    a_ref[...] = (glu[:, :conf] * _sigmoid(glu[:, conf:])).astype(BF16)
    z_ref[...] = _dot(hn, w_z_ref[...]).astype(BF16)
    xbc_ref[...] = _dot(hn, w_xbc_ref[...]).astype(BF16)
    gates_ref[...] = _sigmoid(_dot(hn, w_gates_ref[...])).astype(BF16)

    dt = _softplus(_dot(hn, w_dt_ref[...]) + dtb_ref[...])
    dtt = _softplus(lax.dot_general(w_dtt_ref[...], hn, (((1,), (1,)), ((), ())),
                                    preferred_element_type=F32) + dtbt_ref[...])
    row = lax.broadcasted_iota(jnp.int32, (CHUNK, 1), 0)
    col = lax.broadcasted_iota(jnp.int32, (1, CHUNK), 1)
    for k in range(tm // CHUNK):
        chunk = (pl.program_id(0) * (tm // CHUNK) + k) % chunks_per_frame
        first_live = jnp.where(chunk == chunks_per_frame - 1, META_PAD, 0)
        sl = slice(k * CHUNK, (k + 1) * CHUNK)
        dt_ref[sl, :] = jnp.where(row >= first_live, dt[sl, :], 0.0)
        dtt_ref[:, sl] = jnp.where(col >= first_live, dtt[:, sl], 0.0)


def _inproj(hs, g, w_glu, w_z, w_xbc, w_dt, w_dtt, w_gates, dtb, dtbt, *, tm, chunks_per_frame):
    rows, d = hs.shape
    conf = w_glu.shape[1] // 2
    n_z, n_xbc, n_dt, n_g = w_z.shape[1], w_xbc.shape[1], w_dt.shape[1], w_gates.shape[1]
    row_spec = lambda n: pl.BlockSpec((tm, n), lambda i: (i, 0))
    return pl.pallas_call(
        functools.partial(_inproj_kernel, tm=tm, chunks_per_frame=chunks_per_frame, conf=conf),
        grid=(rows // tm,),
        in_specs=[row_spec(d), _resident(g.shape), _resident(w_glu.shape), _resident(w_z.shape),
                  _resident(w_xbc.shape), _resident(w_dt.shape), _resident(w_dtt.shape),
                  _resident(w_gates.shape), _resident(dtb.shape), _resident(dtbt.shape)],
        out_specs=[row_spec(conf), row_spec(n_z), row_spec(n_xbc), row_spec(n_dt),
                   pl.BlockSpec((n_dt, tm), lambda i: (0, i)), row_spec(n_g)],
        out_shape=[jax.ShapeDtypeStruct((rows, conf), BF16), jax.ShapeDtypeStruct((rows, n_z), BF16),
                   jax.ShapeDtypeStruct((rows, n_xbc), BF16), jax.ShapeDtypeStruct((rows, n_dt), F32),
                   jax.ShapeDtypeStruct((n_dt, rows), F32), jax.ShapeDtypeStruct((rows, n_g), BF16)],
        compiler_params=pltpu.CompilerParams(dimension_semantics=("parallel",),
                                             vmem_limit_bytes=VMEM_LIMIT),
        name="inproj",
    )(hs, g, w_glu, w_z, w_xbc, w_dt, w_dtt, w_gates, dtb, dtbt)


def _fill_window(buf_ref, prev_ref, main_ref, next_ref):
    for t in range(buf_ref.shape[0]):
        sl = slice(t * LANES, (t + 1) * LANES)
        buf_ref[t, 0:HALO, :] = prev_ref[0, :, sl].astype(F32)
        buf_ref[t, HALO:HALO + CHUNK, :] = main_ref[0, :, sl].astype(F32)
        buf_ref[t, HALO + CHUNK:, :] = next_ref[0, :, sl].astype(F32)


def _dwconv_tile(buf_ref, w_ref, b_ref, ksize, t):
    half = (ksize - 1) // 2
    sl = slice(t * LANES, (t + 1) * LANES)
    acc = jnp.broadcast_to(b_ref[:, sl], (CHUNK, LANES))
    for k in range(ksize):
        start = HALO - half + k
        acc = acc + buf_ref[t, start:start + CHUNK, :] * w_ref[k:k + 1, sl]
    return acc


def _conv_a_kernel(prev_ref, main_ref, next_ref, gate_ref, w_ref, b_ref, lng_ref, lnb_ref, wout_ref,
                   o_ref, buf_ref, conv_ref):
    _fill_window(buf_ref, prev_ref, main_ref, next_ref)
    for t in range(buf_ref.shape[0]):
        conv_ref[:, t * LANES:(t + 1) * LANES] = _dwconv_tile(buf_ref, w_ref, b_ref, CONF_KERNEL, t)
    v = conv_ref[...]
    mu = jnp.mean(v, axis=-1, keepdims=True)
    vc = v - mu
    var = jnp.mean(vc * vc, axis=-1, keepdims=True)
    h = _silu(vc * lax.rsqrt(var + EPS) * lng_ref[...] + lnb_ref[...]).astype(BF16)
    y = _dot(h, wout_ref[...])
    o_ref[0] = (gate_ref[0].astype(F32) * y).astype(BF16)


def _halo_maps(n_x_chunks):
    per = CHUNK // HALO
    meta_rows_block = (n_x_chunks * CHUNK + META_PAD) // HALO
    prev_map = lambda b, j: (b, jnp.where(j == 0, meta_rows_block, j * per - 1), 0)
    next_map = lambda b, j: (b, jnp.where(j == n_x_chunks, 0, j * per + per), 0)
    return prev_map, next_map


def _conv_a(a, gates, w, b, lng, lnb, wout, *, n_x_chunks):
    bsz, _, c = a.shape
    prev_map, next_map = _halo_maps(n_x_chunks)
    return pl.pallas_call(
        _conv_a_kernel,
        grid=(bsz, n_x_chunks),
        in_specs=[pl.BlockSpec((1, HALO, c), prev_map),
                  pl.BlockSpec((1, CHUNK, c), lambda b, j: (b, j, 0)),
                  pl.BlockSpec((1, HALO, c), next_map),
                  pl.BlockSpec((1, CHUNK, c), lambda b, j: (b, j, 0)),
                  _resident(w.shape), _resident(b.shape), _resident(lng.shape), _resident(lnb.shape),
                  _resident(wout.shape)],
        out_specs=pl.BlockSpec((1, CHUNK, c), lambda b, j: (b, j, 0)),
        out_shape=jax.ShapeDtypeStruct((bsz, n_x_chunks * CHUNK, c), BF16),
        scratch_shapes=[pltpu.VMEM((c // LANES, CHUNK + 2 * HALO, LANES), F32), pltpu.VMEM((CHUNK, c), F32)],
        compiler_params=pltpu.CompilerParams(dimension_semantics=("parallel", "parallel"),
                                             vmem_limit_bytes=VMEM_LIMIT),
        name="conv_a",
    )(a, a, a, gates, w, b, lng, lnb, wout)


def _conv_s_kernel(prev_ref, main_ref, next_ref, w_ref, b_ref, o_ref, buf_ref):
    _fill_window(buf_ref, prev_ref, main_ref, next_ref)
    for t in range(buf_ref.shape[0]):
        o_ref[0, :, t * LANES:(t + 1) * LANES] = _silu(
            _dwconv_tile(buf_ref, w_ref, b_ref, SSM_CONV, t)).astype(BF16)


def _conv_s(xbc, w, b, *, n_x_chunks):
    bsz, frame, c = xbc.shape
    prev_map, next_map = _halo_maps(n_x_chunks)
    return pl.pallas_call(
        _conv_s_kernel,
        grid=(bsz, n_x_chunks + 1),
        in_specs=[pl.BlockSpec((1, HALO, c), prev_map),
                  pl.BlockSpec((1, CHUNK, c), lambda b, j: (b, j, 0)),
                  pl.BlockSpec((1, HALO, c), next_map),
                  _resident(w.shape), _resident(b.shape)],
        out_specs=pl.BlockSpec((1, CHUNK, c), lambda b, j: (b, j, 0)),
        out_shape=jax.ShapeDtypeStruct((bsz, frame, c), BF16),
        scratch_shapes=[pltpu.VMEM((c // LANES, CHUNK + 2 * HALO, LANES), F32)],
        compiler_params=pltpu.CompilerParams(dimension_semantics=("parallel", "parallel"),
                                             vmem_limit_bytes=VMEM_LIMIT),
        name="conv_s",
    )(xbc, xbc, xbc, w, b)


def _spread_matrix(n):
    return np.kron(np.eye(n, dtype=np.float32), np.ones((1, SSM_HEADDIM), np.float32))


def _triangles():
    li = lax.broadcasted_iota(jnp.int32, (CHUNK, CHUNK), 0)
    si = lax.broadcasted_iota(jnp.int32, (CHUNK, CHUNK), 1)
    return (li >= si).astype(F32), (li <= si).astype(F32), li, si


def _hi_lo_rows(v):
    hi = v.astype(BF16).astype(F32)
    row = lax.broadcasted_iota(jnp.int32, (16, v.shape[1]), 0)
    return jnp.where(row == 0, hi, jnp.where(row == 1, v - hi, 0.0)).astype(BF16)


def _ssd_bwd_state_kernel(x_ref, dt_ref, alog_row_ref, spread_ref, sb_ref, state_ref, *, d_inner, n_heads, hpg):
    @pl.when(pl.program_id(1) == 0)
    def _():
        state_ref[...] = jnp.zeros_like(state_ref)

    gp = hpg * SSM_HEADDIM
    _, upper, _, _ = _triangles()
    dt_c = dt_ref[0, :, n_heads:]
    cum = _dot_exact(upper, dt_c * -jnp.exp(alog_row_ref[:, n_heads:]))
    tot = cum[0:1, :]
    to_edge = jnp.exp(tot - cum) * dt_c
    wide = _dot(jnp.concatenate([to_edge.astype(BF16), _hi_lo_rows(jnp.exp(tot))], axis=0), spread_ref[...])
    for g in range(SSM_GROUPS):
        lanes = slice(g * gp, (g + 1) * gp)
        bm = x_ref[0, :, d_inner + g * D_STATE:d_inner + (g + 1) * D_STATE]
        weighted = (x_ref[0, :, lanes].astype(F32) * wide[:CHUNK, lanes]).astype(BF16)
        s_local = lax.dot_general(bm, weighted, (((0,), (0,)), ((), ())), preferred_element_type=F32)
        s_in = state_ref[g]
        sb_ref[0, 0, g * D_STATE:(g + 1) * D_STATE, :] = s_in.astype(sb_ref.dtype)
        state_ref[g] = (wide[CHUNK:CHUNK + 1, lanes] + wide[CHUNK + 1:CHUNK + 2, lanes]) * s_in + s_local


def _ssd_main_kernel(x_ref, dt_ref, dtt_ref, sb_ref, alog_row_ref, alog_col_ref, dskip_ref, spread_ref,
                     y_ref, sf_ref, *, d_inner, n_heads, hpg):
    @pl.when(pl.program_id(1) == 0)
    def _():
        sf_ref[...] = jnp.zeros_like(sf_ref)

    gp = hpg * SSM_HEADDIM
    lower, upper, li, si = _triangles()
    fwd_sees = li >= si
    on_diag = li == si
    left_head = lax.broadcasted_iota(jnp.int32, (CHUNK, 2 * SSM_HEADDIM), 1) < SSM_HEADDIM

    dt_c = dt_ref[0]
    dt_r = dtt_ref[...]
    adt_c = dt_c * -jnp.exp(alog_row_ref[...])
    adt_r = dt_r * -jnp.exp(alog_col_ref[...])
    cumf_c = _dot_exact(lower, adt_c[:, :n_heads])
    cumb_c = _dot_exact(upper, adt_c[:, n_heads:])
    rowf = _dot_exact(adt_r[:n_heads], upper) - jnp.log(dt_r[:n_heads])
    rowb = _dot_exact(adt_r[n_heads:], lower) - jnp.log(dt_r[n_heads:])
    tot_f = cumf_c[CHUNK - 1:CHUNK, :]
    to_edge_f = jnp.exp(tot_f - cumf_c) * dt_c[:, :n_heads]

    cbs, diag_cols = [], []
    for g in range(SSM_GROUPS):
        bm = x_ref[0, :, d_inner + g * D_STATE:d_inner + (g + 1) * D_STATE]
        cm = x_ref[0, :, d_inner + (SSM_GROUPS + g) * D_STATE:d_inner + (SSM_GROUPS + g + 1) * D_STATE]
        cb = lax.dot_general(cm, bm, (((1,), (1,)), ((), ())), preferred_element_type=F32)
        cbs.append(cb)
        cb_diag = jnp.sum(jnp.where(on_diag, cb, 0.0), axis=1, keepdims=True)
        diag_cols.append(cb_diag * dt_c[:, n_heads + g * hpg:n_heads + (g + 1) * hpg])
    cols = jnp.concatenate([to_edge_f, jnp.exp(cumf_c), jnp.exp(cumb_c)] + diag_cols, axis=1)
    zeros = jnp.zeros((1, n_heads), F32)
    extra = _hi_lo_rows(jnp.concatenate([zeros, jnp.exp(tot_f), zeros, zeros], axis=1))
    wide = _dot(jnp.concatenate([cols.astype(BF16), extra], axis=0), spread_ref[...])

    for g in range(SSM_GROUPS):
        lanes = lambda q: slice(q * d_inner + g * gp, q * d_inner + (g + 1) * gp)
        bm = x_ref[0, :, d_inner + g * D_STATE:d_inner + (g + 1) * D_STATE]
        cm = x_ref[0, :, d_inner + (SSM_GROUPS + g) * D_STATE:d_inner + (SSM_GROUPS + g + 1) * D_STATE]
        xf = x_ref[0, :, g * gp:(g + 1) * gp].astype(F32)
        s_in = sf_ref[g]
        y_off = _dot(cm, jnp.concatenate([s_in.astype(BF16), sb_ref[0, 0, g * D_STATE:(g + 1) * D_STATE, :]], axis=1))
        weighted = (xf * wide[:CHUNK, lanes(0)]).astype(BF16)
        s_local = lax.dot_general(bm, weighted, (((0,), (0,)), ((), ())), preferred_element_type=F32)
        sf_ref[g] = (wide[CHUNK:CHUNK + 1, lanes(1)] + wide[CHUNK + 1:CHUNK + 2, lanes(1)]) * s_in + s_local
        y = (y_off[:, :gp] * wide[:CHUNK, lanes(1)] + y_off[:, gp:] * wide[:CHUNK, lanes(2)]
             + xf * (dskip_ref[:, g * gp:(g + 1) * gp] + wide[:CHUNK, lanes(3)]))

        pairs = []
        for j in range(hpg // 2):
            ms = []
            for h in (g * hpg + 2 * j, g * hpg + 2 * j + 1):
                df = jnp.broadcast_to(cumf_c[:, h:h + 1], (CHUNK, CHUNK)) - jnp.broadcast_to(rowf[h:h + 1, :], (CHUNK, CHUNK))
                db = jnp.broadcast_to(cumb_c[:, h:h + 1], (CHUNK, CHUNK)) - jnp.broadcast_to(rowb[h:h + 1, :], (CHUNK, CHUNK))
                ms.append((cbs[g] * jnp.exp(jnp.where(fwd_sees, df, db))).astype(BF16))
            xp = x_ref[0, :, g * gp + 2 * j * SSM_HEADDIM:g * gp + (2 * j + 2) * SSM_HEADDIM]
            zero = jnp.zeros_like(xp)
            block_diag = jnp.concatenate([jnp.where(left_head, xp, zero), jnp.where(left_head, zero, xp)], axis=0)
            pairs.append(_dot(jnp.concatenate(ms, axis=1), block_diag))
        y_ref[0, :, g * gp:(g + 1) * gp] = (y + jnp.concatenate(pairs, axis=1)).astype(y_ref.dtype)


def _ssd(xbc, dt, dtt, a_log, dskip, *, n_x_chunks, d_inner):
    bsz, frame, c = xbc.shape
    n_heads = dt.shape[2] // 2
    hpg = n_heads // SSM_GROUPS
    gp = hpg * SSM_HEADDIM
    nc = n_x_chunks + 1
    fwd = lambda i: (i + n_x_chunks) % nc
    bwd = lambda i: jnp.where(i == n_x_chunks, n_x_chunks, n_x_chunks - 1 - i)
    row_block = lambda n, order: pl.BlockSpec((1, CHUNK, n), lambda b, i: (b, order(i), 0))
    state_block = lambda order: pl.BlockSpec((1, 1, SSM_GROUPS * D_STATE, gp), lambda b, i: (b, order(i), 0, 0))
    params = pltpu.CompilerParams(dimension_semantics=("parallel", "arbitrary"), vmem_limit_bytes=VMEM_LIMIT)
    state_scratch = pltpu.VMEM((SSM_GROUPS, D_STATE, gp), F32)
    alog_row, alog_col = a_log[None, :], a_log[:, None]
    spread1 = jnp.asarray(_spread_matrix(n_heads), BF16)
    spread4 = jnp.asarray(_spread_matrix(4 * n_heads), BF16)

    sb = pl.pallas_call(
        functools.partial(_ssd_bwd_state_kernel, d_inner=d_inner, n_heads=n_heads, hpg=hpg),
        grid=(bsz, nc),
        in_specs=[row_block(c, bwd), row_block(2 * n_heads, bwd), _resident(alog_row.shape),
                  _resident(spread1.shape)],
        out_specs=state_block(bwd),
        out_shape=jax.ShapeDtypeStruct((bsz, nc, SSM_GROUPS * D_STATE, gp), BF16),
        scratch_shapes=[state_scratch],
        compiler_params=params,
        name="ssd_bwd_state",
    )(xbc, dt, alog_row, spread1)

    return pl.pallas_call(
        functools.partial(_ssd_main_kernel, d_inner=d_inner, n_heads=n_heads, hpg=hpg),
        grid=(bsz, nc),
        in_specs=[row_block(c, fwd), row_block(2 * n_heads, fwd),
                  pl.BlockSpec((2 * n_heads, CHUNK), lambda b, i: (0, b * nc + fwd(i))), state_block(fwd),
                  _resident(alog_row.shape), _resident(alog_col.shape), _resident(dskip.shape),
                  _resident(spread4.shape)],
        out_specs=row_block(d_inner, fwd),
        out_shape=jax.ShapeDtypeStruct((bsz, frame, d_inner), BF16),
        scratch_shapes=[state_scratch],
        compiler_params=params,
        name="ssd_main",
    )(xbc, dt, dtt, sb, alog_row, alog_col, dskip, spread4)


def _tail_kernel(x_ref, y_ref, z_ref, gate_ref, yc_ref, nw_ref, wso_ref, wo_ref, nffn_ref,
                 wg_ref, wu_ref, wd_ref, nfin_ref, o_ref):
    d_inner = y_ref.shape[2]
    gc = d_inner // SSM_GROUPS
    v = y_ref[0].astype(F32) * _silu(z_ref[0].astype(F32))
    parts = []
    for g in range(SSM_GROUPS):
        vg = v[:, g * gc:(g + 1) * gc]
        ms = jnp.mean(vg * vg, axis=-1, keepdims=True)
        parts.append((vg * lax.rsqrt(ms + EPS) * nw_ref[:, g * gc:(g + 1) * gc]).astype(BF16))
    y_ssm = _dot(jnp.concatenate(parts, axis=1), wso_ref[...])
    merged = yc_ref[0].astype(F32) + gate_ref[0].astype(F32) * y_ssm
    hs = x_ref[0] + _dot(merged.astype(BF16), wo_ref[...])

    ms = jnp.mean(hs * hs, axis=-1, keepdims=True)
    hn = (hs * lax.rsqrt(ms + EPS) * nffn_ref[...]).astype(BF16)
    act = (_silu(_dot(hn, wg_ref[...])) * _dot(hn, wu_ref[...])).astype(BF16)
    hs = hs + _dot(act, wd_ref[...])

    ms = jnp.mean(hs * hs, axis=-1, keepdims=True)
    o_ref[0] = hs * lax.rsqrt(ms + EPS) * nfin_ref[...]


def _tail(x, y, z, gates, yc, nw, wso, wo, nffn, wg, wu, wd, nfin, *, tm):
    bsz, seq, d = x.shape
    d_inner = y.shape[2]
    blk = lambda n, col=0: pl.BlockSpec((1, tm, n), lambda b, j: (b, j, col))
    return pl.pallas_call(
        _tail_kernel,
        grid=(bsz, seq // tm),
        in_specs=[blk(d), blk(d_inner), blk(d_inner), blk(d, 1), blk(d),
                  _resident(nw.shape), _resident(wso.shape), _resident(wo.shape), _resident(nffn.shape),
                  _resident(wg.shape), _resident(wu.shape), _resident(wd.shape), _resident(nfin.shape)],
        out_specs=blk(d),
        out_shape=jax.ShapeDtypeStruct((bsz, seq, d), F32),
        compiler_params=pltpu.CompilerParams(dimension_semantics=("parallel", "parallel"),
                                             vmem_limit_bytes=VMEM_LIMIT),
        name="tail",
    )(x, y, z, gates, yc, nw, wso, wo, nffn, wg, wu, wd, nfin)


def kernel(x, meta_tokens, norm_mix, w_in, conv_dw_w, conv_dw_b, conv_ln_g, conv_ln_b, conv_out_w,
           ssm_conv_w, ssm_conv_b, dt_bias_f, dt_bias_b, a_log_f, a_log_b, ssm_d, ssm_norm_w, ssm_out_w,
           w_o, norm_ffn, w_gate, w_up, w_down, norm_final):
    bsz, seq, d = x.shape
    assert norm_mix.shape[0] == 1 and seq % CHUNK == 0 and meta_tokens.shape[0] == N_META
    conf = conv_dw_w.shape[2]
    d_inner = ssm_norm_w.shape[1]
    n_heads = ssm_d.shape[1]
    gn = SSM_GROUPS * D_STATE
    n_x_chunks = seq // CHUNK
    frame = seq + CHUNK
    rows = bsz * frame

    hs = jnp.concatenate([x, jnp.zeros((bsz, META_PAD, d), x.dtype),
                          jnp.broadcast_to(meta_tokens[None].astype(x.dtype), (bsz, N_META, d))], axis=1)

    w = w_in[0]
    o_z = 2 * conf
    o_xbc = o_z + d_inner
    o_dt = o_xbc + d_inner + 2 * gn
    o_g = o_dt + 2 * n_heads
    w_dt = w[:, o_dt:o_g]
    dt_bias = jnp.concatenate([dt_bias_f[0], dt_bias_b[0]])
    a_log = jnp.concatenate([a_log_f[0], a_log_b[0]])

    a, z, xbc, dt, dtt, gates = _inproj(
        hs.reshape(rows, d), norm_mix, w[:, :o_z].astype(BF16), w[:, o_z:o_xbc].astype(BF16),
        w[:, o_xbc:o_dt].astype(BF16), w_dt.astype(BF16), w_dt.T.astype(BF16), w[:, o_g:].astype(BF16),
        dt_bias[None, :], dt_bias[:, None], tm=2 * CHUNK, chunks_per_frame=n_x_chunks + 1)
    a = a.reshape(bsz, frame, conf)
    z = z.reshape(bsz, frame, d_inner)
    xbc = xbc.reshape(bsz, frame, d_inner + 2 * gn)
    dt = dt.reshape(bsz, frame, 2 * n_heads)
    gates = gates.reshape(bsz, frame, 2 * d)

    yc = _conv_a(a, gates, conv_dw_w[0], conv_dw_b, conv_ln_g, conv_ln_b, conv_out_w[0].astype(BF16),
                 n_x_chunks=n_x_chunks)
    xbc = _conv_s(xbc, ssm_conv_w[0], ssm_conv_b, n_x_chunks=n_x_chunks)
    y = _ssd(xbc, dt, dtt, a_log, jnp.repeat(ssm_d[0], SSM_HEADDIM)[None, :],
             n_x_chunks=n_x_chunks, d_inner=d_inner)
    return _tail(x, y, z, gates, yc, ssm_norm_w, ssm_out_w[0].astype(BF16), w_o[0].astype(BF16),
                 norm_ffn, w_gate[0].astype(BF16), w_up[0].astype(BF16), w_down[0].astype(BF16),
                 norm_final[None, :], tm=2 * CHUNK)
```

```python
import functools

import jax
import jax.numpy as jnp
import numpy as np
from jax import lax
from jax.experimental import pallas as pl
from jax.experimental.pallas import tpu as pltpu

F32 = jnp.float32
BF16 = jnp.bfloat16

N_META = 16
CHUNK = 128
META_PAD = CHUNK - N_META
CONF_KERNEL = 31
SSM_CONV = 7
SSM_HEADDIM = 64
SSM_GROUPS = 4
D_STATE = 128
EPS = 1e-6
LANES = 128
HALO = 16
ROW_TILE = 2 * CHUNK
VMEM_LIMIT = 56 * 1024 * 1024


def _sigmoid(v):
    return 1.0 / (1.0 + jnp.exp(-v))


def _silu(v):
    return v * _sigmoid(v)


def _softplus(v):
    return jnp.maximum(v, 0.0) + jnp.log1p(jnp.exp(-jnp.abs(v)))


def _dot(a, b):
    return jnp.dot(a, b, preferred_element_type=F32)


def _dot_exact(a, b):
    return jnp.dot(a, b, preferred_element_type=F32, precision=lax.Precision.HIGHEST)


def _resident(shape):
    nd = len(shape)
    return pl.BlockSpec(shape, lambda *_: (0,) * nd, pipeline_mode=pl.Buffered(1))


def _dwconv_tile(buf_ref, w_ref, b_ref, ksize, t, rows):
    half = (ksize - 1) // 2
    sl = slice(t * LANES, (t + 1) * LANES)
    acc = jnp.broadcast_to(b_ref[:, sl], (rows, LANES))
    for k in range(ksize):
        start = HALO - half + k
        acc = acc + buf_ref[t, start:start + rows, :] * w_ref[k:k + 1, sl]
    return acc


def _inproj_kernel(prev_ref, main_ref, next_ref, meta_ref, g_ref, w_glu_ref, w_z_ref, w_xbc_ref, w_dt_ref,
                   w_dtt_ref, w_gates_ref, dtb_ref, dtbt_ref, cw_ref, cb_ref,
                   a_ref, z_ref, xbc_ref, dt_ref, dtt_ref, gates_ref, hs_ref, buf_ref, *, n_x_tiles, conf):
    tm = main_ref.shape[1]
    j = pl.program_id(1)

    @pl.when(j < n_x_tiles)
    def _():
        hs_ref[0:HALO, :] = jnp.where(j == 0, meta_ref[META_PAD:CHUNK, :], prev_ref[0])
        hs_ref[HALO:HALO + tm, :] = main_ref[0]
        hs_ref[HALO + tm:, :] = jnp.where(j == n_x_tiles - 1, 0.0, next_ref[0])

    @pl.when(j == n_x_tiles)
    def _():
        hs_ref[0:HALO, :] = jnp.zeros((HALO, hs_ref.shape[1]), F32)
        hs_ref[HALO:HALO + CHUNK, :] = meta_ref[...]
        hs_ref[HALO + CHUNK:HALO + CHUNK + HALO, :] = next_ref[0]
        hs_ref[HALO + CHUNK + HALO:, :] = jnp.zeros((tm - CHUNK, hs_ref.shape[1]), F32)

    x = hs_ref[...]
    ms = jnp.mean(x * x, axis=-1, keepdims=True)
    hn_ext = (x * lax.rsqrt(ms + EPS) * g_ref[...]).astype(BF16)
    hn = hn_ext[HALO:HALO + tm]

    n_xbc = w_xbc_ref.shape[1]
    for c0 in range(0, n_xbc, 2 * LANES):
        r = _dot(hn_ext, w_xbc_ref[:, c0:c0 + 2 * LANES])
        buf_ref[c0 // LANES] = r[:, :LANES]
        buf_ref[c0 // LANES + 1] = r[:, LANES:]
    for t in range(n_xbc // LANES):
        xbc_ref[0, :, t * LANES:(t + 1) * LANES] = _silu(
            _dwconv_tile(buf_ref, cw_ref, cb_ref, SSM_CONV, t, tm)).astype(BF16)

    glu = _dot(hn, w_glu_ref[...])
    a_ref[0] = (glu[:, :conf] * _sigmoid(glu[:, conf:])).astype(BF16)
    z_ref[0] = _dot(hn, w_z_ref[...]).astype(BF16)
    gates_ref[0] = _sigmoid(_dot(hn, w_gates_ref[...])).astype(BF16)

    dt = _softplus(_dot(hn, w_dt_ref[...]) + dtb_ref[...])
    dtt = _softplus(lax.dot_general(w_dtt_ref[...], hn, (((1,), (1,)), ((), ())),
                                    preferred_element_type=F32) + dtbt_ref[...])
    first_live = jnp.where(j == n_x_tiles, META_PAD, 0)
    dt_ref[0] = jnp.where(lax.broadcasted_iota(jnp.int32, (tm, 1), 0) >= first_live, dt, 0.0)
    dtt_ref[0] = jnp.where(lax.broadcasted_iota(jnp.int32, (1, tm), 1) >= first_live, dtt, 0.0)


def _inproj(x, meta_chunk, g, w_glu, w_z, w_xbc, w_dt, w_dtt, w_gates, dtb, dtbt, cw, cb, *, tm):
    bsz, seq, d = x.shape
    frame = seq + CHUNK
    n_x_tiles = seq // tm
    per = tm // HALO
    conf = w_glu.shape[1] // 2
    n_z, n_xbc, n_dt, n_g = w_z.shape[1], w_xbc.shape[1], w_dt.shape[1], w_gates.shape[1]
    last = n_x_tiles - 1
    main_map = lambda b, j: (b, jnp.minimum(j, last), 0)
    prev_map = lambda b, j: (b, jnp.where(jnp.logical_or(j == 0, j > last), 0, j * per - 1), 0)
    next_map = lambda b, j: (b, jnp.where(j >= last, 0, (j + 1) * per), 0)
    out_spec = lambda n: pl.BlockSpec((1, tm, n), lambda b, j: (b, j, 0))
    weights = [g, w_glu, w_z, w_xbc, w_dt, w_dtt, w_gates, dtb, dtbt, cw, cb]
    return pl.pallas_call(
        functools.partial(_inproj_kernel, n_x_tiles=n_x_tiles, conf=conf),
        grid=(bsz, n_x_tiles + 1),
        in_specs=[pl.BlockSpec((1, HALO, d), prev_map), pl.BlockSpec((1, tm, d), main_map),
                  pl.BlockSpec((1, HALO, d), next_map), _resident(meta_chunk.shape)]
                 + [_resident(w.shape) for w in weights],
        out_specs=[out_spec(conf), out_spec(n_z), out_spec(n_xbc), out_spec(n_dt),
                   pl.BlockSpec((1, n_dt, tm), lambda b, j: (b, 0, j)), out_spec(n_g)],
        out_shape=[jax.ShapeDtypeStruct((bsz, frame, conf), BF16), jax.ShapeDtypeStruct((bsz, frame, n_z), BF16),
                   jax.ShapeDtypeStruct((bsz, frame, n_xbc), BF16), jax.ShapeDtypeStruct((bsz, frame, n_dt), F32),
                   jax.ShapeDtypeStruct((bsz, n_dt, frame), F32), jax.ShapeDtypeStruct((bsz, frame, n_g), BF16)],
        scratch_shapes=[pltpu.VMEM((tm + 2 * HALO, d), F32),
                        pltpu.VMEM((n_xbc // LANES, tm + 2 * HALO, LANES), F32)],
        compiler_params=pltpu.CompilerParams(dimension_semantics=("parallel", "parallel"),
                                             vmem_limit_bytes=VMEM_LIMIT),
        name="inproj",
    )(x, x, x, meta_chunk, *weights)


def _spread_matrix(n):
    return np.kron(np.eye(n, dtype=np.float32), np.ones((1, SSM_HEADDIM), np.float32))


def _triangles():
    li = lax.broadcasted_iota(jnp.int32, (CHUNK, CHUNK), 0)
    si = lax.broadcasted_iota(jnp.int32, (CHUNK, CHUNK), 1)
    return (li >= si).astype(F32), (li <= si).astype(F32), li, si


def _hi_lo_rows(v):
    hi = v.astype(BF16).astype(F32)
    row = lax.broadcasted_iota(jnp.int32, (16, v.shape[1]), 0)
    return jnp.where(row == 0, hi, jnp.where(row == 1, v - hi, 0.0)).astype(BF16)


def _ssd_bwd_state_kernel(x_ref, dt_ref, alog_row_ref, spread_ref, sb_ref, state_ref, *, d_inner, n_heads, hpg):
    @pl.when(pl.program_id(1) == 0)
    def _():
        state_ref[...] = jnp.zeros_like(state_ref)

    gp = hpg * SSM_HEADDIM
    _, upper, _, _ = _triangles()
    dt_c = dt_ref[0, :, n_heads:]
    cum = _dot_exact(upper, dt_c * -jnp.exp(alog_row_ref[:, n_heads:]))
    tot = cum[0:1, :]
    to_edge = jnp.exp(tot - cum) * dt_c
    wide = _dot(jnp.concatenate([to_edge.astype(BF16), _hi_lo_rows(jnp.exp(tot))], axis=0), spread_ref[...])
    for g in range(SSM_GROUPS):
        lanes = slice(g * gp, (g + 1) * gp)
        bm = x_ref[0, :, d_inner + g * D_STATE:d_inner + (g + 1) * D_STATE]
        weighted = (x_ref[0, :, lanes].astype(F32) * wide[:CHUNK, lanes]).astype(BF16)
        s_local = lax.dot_general(bm, weighted, (((0,), (0,)), ((), ())), preferred_element_type=F32)
        s_in = state_ref[g]
        sb_ref[0, 0, g * D_STATE:(g + 1) * D_STATE, :] = s_in.astype(sb_ref.dtype)
        state_ref[g] = (wide[CHUNK:CHUNK + 1, lanes] + wide[CHUNK + 1:CHUNK + 2, lanes]) * s_in + s_local


def _ssd_main_kernel(x_ref, dt_ref, dtt_ref, sb_ref, alog_row_ref, alog_col_ref, dskip_ref, spread_ref,
                     y_ref, sf_ref, *, d_inner, n_heads, hpg):
    @pl.when(pl.program_id(1) == 0)
    def _():
        sf_ref[...] = jnp.zeros_like(sf_ref)

    gp = hpg * SSM_HEADDIM
    lower, upper, li, si = _triangles()
    fwd_sees = li >= si
    on_diag = li == si
    left_head = lax.broadcasted_iota(jnp.int32, (CHUNK, 2 * SSM_HEADDIM), 1) < SSM_HEADDIM

    dt_c = dt_ref[0]
    dt_r = dtt_ref[0]
    adt_c = dt_c * -jnp.exp(alog_row_ref[...])
    adt_r = dt_r * -jnp.exp(alog_col_ref[...])
    cumf_c = _dot_exact(lower, adt_c[:, :n_heads])
    cumb_c = _dot_exact(upper, adt_c[:, n_heads:])
    rowf = _dot_exact(adt_r[:n_heads], upper) - jnp.log(dt_r[:n_heads])
    rowb = _dot_exact(adt_r[n_heads:], lower) - jnp.log(dt_r[n_heads:])
    tot_f = cumf_c[CHUNK - 1:CHUNK, :]
    to_edge_f = jnp.exp(tot_f - cumf_c) * dt_c[:, :n_heads]

    cbs, diag_cols = [], []
    for g in range(SSM_GROUPS):
        bm = x_ref[0, :, d_inner + g * D_STATE:d_inner + (g + 1) * D_STATE]
        cm = x_ref[0, :, d_inner + (SSM_GROUPS + g) * D_STATE:d_inner + (SSM_GROUPS + g + 1) * D_STATE]
        cb = lax.dot_general(cm, bm, (((1,), (1,)), ((), ())), preferred_element_type=F32)
        cbs.append(cb)
        cb_diag = jnp.sum(jnp.where(on_diag, cb, 0.0), axis=1, keepdims=True)
        diag_cols.append(cb_diag * dt_c[:, n_heads + g * hpg:n_heads + (g + 1) * hpg])
    cols = jnp.concatenate([to_edge_f, jnp.exp(cumf_c), jnp.exp(cumb_c)] + diag_cols, axis=1)
    zeros = jnp.zeros((1, n_heads), F32)
    extra = _hi_lo_rows(jnp.concatenate([zeros, jnp.exp(tot_f), zeros, zeros], axis=1))
    wide = _dot(jnp.concatenate([cols.astype(BF16), extra], axis=0), spread_ref[...])

    for g in range(SSM_GROUPS):
        lanes = lambda q: slice(q * d_inner + g * gp, q * d_inner + (g + 1) * gp)
        bm = x_ref[0, :, d_inner + g * D_STATE:d_inner + (g + 1) * D_STATE]
        cm = x_ref[0, :, d_inner + (SSM_GROUPS + g) * D_STATE:d_inner + (SSM_GROUPS + g + 1) * D_STATE]
        xf = x_ref[0, :, g * gp:(g + 1) * gp].astype(F32)
        s_in = sf_ref[g]
        y_off = _dot(cm, jnp.concatenate([s_in.astype(BF16), sb_ref[0, 0, g * D_STATE:(g + 1) * D_STATE, :]], axis=1))
        weighted = (xf * wide[:CHUNK, lanes(0)]).astype(BF16)
        s_local = lax.dot_general(bm, weighted, (((0,), (0,)), ((), ())), preferred_element_type=F32)
        sf_ref[g] = (wide[CHUNK:CHUNK + 1, lanes(1)] + wide[CHUNK + 1:CHUNK + 2, lanes(1)]) * s_in + s_local
        y = (y_off[:, :gp] * wide[:CHUNK, lanes(1)] + y_off[:, gp:] * wide[:CHUNK, lanes(2)]
             + xf * (dskip_ref[:, g * gp:(g + 1) * gp] + wide[:CHUNK, lanes(3)]))

        pairs = []
        for j in range(hpg // 2):
            ms = []
            for h in (g * hpg + 2 * j, g * hpg + 2 * j + 1):
                df = jnp.broadcast_to(cumf_c[:, h:h + 1], (CHUNK, CHUNK)) - jnp.broadcast_to(rowf[h:h + 1, :], (CHUNK, CHUNK))
                db = jnp.broadcast_to(cumb_c[:, h:h + 1], (CHUNK, CHUNK)) - jnp.broadcast_to(rowb[h:h + 1, :], (CHUNK, CHUNK))
                ms.append((cbs[g] * jnp.exp(jnp.where(fwd_sees, df, db))).astype(BF16))
            xp = x_ref[0, :, g * gp + 2 * j * SSM_HEADDIM:g * gp + (2 * j + 2) * SSM_HEADDIM]
            zero = jnp.zeros_like(xp)
            block_diag = jnp.concatenate([jnp.where(left_head, xp, zero), jnp.where(left_head, zero, xp)], axis=0)
            pairs.append(_dot(jnp.concatenate(ms, axis=1), block_diag))
        y_ref[0, :, g * gp:(g + 1) * gp] = (y + jnp.concatenate(pairs, axis=1)).astype(y_ref.dtype)


def _ssd(xbc, dt, dtt, a_log, dskip, *, n_x_chunks, d_inner):
    bsz, frame, c = xbc.shape
    n_heads = dt.shape[2] // 2
    hpg = n_heads // SSM_GROUPS
    gp = hpg * SSM_HEADDIM
    nc = n_x_chunks + 1
    fwd = lambda i: (i + n_x_chunks) % nc
    bwd = lambda i: jnp.where(i == n_x_chunks, n_x_chunks, n_x_chunks - 1 - i)
    row_block = lambda n, order: pl.BlockSpec((1, CHUNK, n), lambda b, i: (b, order(i), 0))
    state_block = lambda order: pl.BlockSpec((1, 1, SSM_GROUPS * D_STATE, gp), lambda b, i: (b, order(i), 0, 0))
    params = pltpu.CompilerParams(dimension_semantics=("parallel", "arbitrary"), vmem_limit_bytes=VMEM_LIMIT)
    state_scratch = pltpu.VMEM((SSM_GROUPS, D_STATE, gp), F32)
    alog_row, alog_col = a_log[None, :], a_log[:, None]
    spread1 = jnp.asarray(_spread_matrix(n_heads), BF16)
    spread4 = jnp.asarray(_spread_matrix(4 * n_heads), BF16)

    sb = pl.pallas_call(
        functools.partial(_ssd_bwd_state_kernel, d_inner=d_inner, n_heads=n_heads, hpg=hpg),
        grid=(bsz, nc),
        in_specs=[row_block(c, bwd), row_block(2 * n_heads, bwd), _resident(alog_row.shape),
                  _resident(spread1.shape)],
        out_specs=state_block(bwd),
        out_shape=jax.ShapeDtypeStruct((bsz, nc, SSM_GROUPS * D_STATE, gp), BF16),
        scratch_shapes=[state_scratch],
        compiler_params=params,
        name="ssd_bwd_state",
    )(xbc, dt, alog_row, spread1)

    return pl.pallas_call(
        functools.partial(_ssd_main_kernel, d_inner=d_inner, n_heads=n_heads, hpg=hpg),
        grid=(bsz, nc),
        in_specs=[row_block(c, fwd), row_block(2 * n_heads, fwd),
                  pl.BlockSpec((1, 2 * n_heads, CHUNK), lambda b, i: (b, 0, fwd(i))), state_block(fwd),
                  _resident(alog_row.shape), _resident(alog_col.shape), _resident(dskip.shape),
                  _resident(spread4.shape)],
        out_specs=row_block(d_inner, fwd),
        out_shape=jax.ShapeDtypeStruct((bsz, frame, d_inner), BF16),
        scratch_shapes=[state_scratch],
        compiler_params=params,
        name="ssd_main",
    )(xbc, dt, dtt, sb, alog_row, alog_col, dskip, spread4)


def _tail_kernel(x_ref, y_ref, z_ref, gates_ref, a_prev_ref, a_ref, a_next_ref,
                 cw_ref, cb_ref, lng_ref, lnb_ref, wco_ref, nw_ref, wso_ref, wo_ref, nffn_ref,
                 wg_ref, wu_ref, wd_ref, nfin_ref, o_ref, buf_ref, conv_ref):
    tm, d = x_ref.shape[1], x_ref.shape[2]
    d_inner = y_ref.shape[2]
    gc = d_inner // SSM_GROUPS

    for t in range(buf_ref.shape[0]):
        sl = slice(t * LANES, (t + 1) * LANES)
        buf_ref[t, 0:HALO, :] = a_prev_ref[0, :, sl].astype(F32)
        buf_ref[t, HALO:HALO + tm, :] = a_ref[0, :, sl].astype(F32)
        buf_ref[t, HALO + tm:, :] = a_next_ref[0, :, sl].astype(F32)
    for t in range(buf_ref.shape[0]):
        conv_ref[:, t * LANES:(t + 1) * LANES] = _dwconv_tile(buf_ref, cw_ref, cb_ref, CONF_KERNEL, t, tm)
    v = conv_ref[...]
    mu = jnp.mean(v, axis=-1, keepdims=True)
    vc = v - mu
    var = jnp.mean(vc * vc, axis=-1, keepdims=True)
    h = _silu(vc * lax.rsqrt(var + EPS) * lng_ref[...] + lnb_ref[...]).astype(BF16)
    y_conv = _dot(h, wco_ref[...])

    v = y_ref[0].astype(F32) * _silu(z_ref[0].astype(F32))
    parts = []
    for g in range(SSM_GROUPS):
        vg = v[:, g * gc:(g + 1) * gc]
        ms = jnp.mean(vg * vg, axis=-1, keepdims=True)
        parts.append((vg * lax.rsqrt(ms + EPS) * nw_ref[:, g * gc:(g + 1) * gc]).astype(BF16))
    y_ssm = _dot(jnp.concatenate(parts, axis=1), wso_ref[...])

    merged = gates_ref[0, :, :d].astype(F32) * y_conv + gates_ref[0, :, d:].astype(F32) * y_ssm
    hs = x_ref[0] + _dot(merged.astype(BF16), wo_ref[...])

    ms = jnp.mean(hs * hs, axis=-1, keepdims=True)
    hn = (hs * lax.rsqrt(ms + EPS) * nffn_ref[...]).astype(BF16)
    act = (_silu(_dot(hn, wg_ref[...])) * _dot(hn, wu_ref[...])).astype(BF16)
    hs = hs + _dot(act, wd_ref[...])

    ms = jnp.mean(hs * hs, axis=-1, keepdims=True)
    o_ref[0] = hs * lax.rsqrt(ms + EPS) * nfin_ref[...]


def _tail(x, y, z, gates, a, weights, *, tm):
    bsz, seq, d = x.shape
    d_inner = y.shape[2]
    conf = a.shape[2]
    per = tm // HALO
    meta_rows_block = (seq + META_PAD) // HALO
    blk = lambda n: pl.BlockSpec((1, tm, n), lambda b, j: (b, j, 0))
    prev_map = lambda b, j: (b, jnp.where(j == 0, meta_rows_block, j * per - 1), 0)
    next_map = lambda b, j: (b, (j + 1) * per, 0)
    return pl.pallas_call(
        _tail_kernel,
        grid=(bsz, seq // tm),
        in_specs=[blk(d), blk(d_inner), blk(d_inner), blk(2 * d),
                  pl.BlockSpec((1, HALO, conf), prev_map), blk(conf), pl.BlockSpec((1, HALO, conf), next_map)]
                 + [_resident(w.shape) for w in weights],
        out_specs=blk(d),
        out_shape=jax.ShapeDtypeStruct((bsz, seq, d), F32),
        scratch_shapes=[pltpu.VMEM((conf // LANES, tm + 2 * HALO, LANES), F32), pltpu.VMEM((tm, conf), F32)],
        compiler_params=pltpu.CompilerParams(dimension_semantics=("parallel", "parallel"),
                                             vmem_limit_bytes=VMEM_LIMIT),
        name="tail",
    )(x, y, z, gates, a, a, a, *weights)


def kernel(x, meta_tokens, norm_mix, w_in, conv_dw_w, conv_dw_b, conv_ln_g, conv_ln_b, conv_out_w,
           ssm_conv_w, ssm_conv_b, dt_bias_f, dt_bias_b, a_log_f, a_log_b, ssm_d, ssm_norm_w, ssm_out_w,
           w_o, norm_ffn, w_gate, w_up, w_down, norm_final):
    bsz, seq, d = x.shape
    assert norm_mix.shape[0] == 1 and seq % ROW_TILE == 0 and meta_tokens.shape[0] == N_META
    conf = conv_dw_w.shape[2]
    d_inner = ssm_norm_w.shape[1]
    n_heads = ssm_d.shape[1]
    gn = SSM_GROUPS * D_STATE
    assert conf == d

    w = w_in[0]
    o_z = 2 * conf
    o_xbc = o_z + d_inner
    o_dt = o_xbc + d_inner + 2 * gn
    o_g = o_dt + 2 * n_heads
    w_dt = w[:, o_dt:o_g]
    dt_bias = jnp.concatenate([dt_bias_f[0], dt_bias_b[0]])
    a_log = jnp.concatenate([a_log_f[0], a_log_b[0]])
    meta_chunk = jnp.concatenate([jnp.zeros((META_PAD, d), x.dtype), meta_tokens.astype(x.dtype)], axis=0)

    a, z, xbc, dt, dtt, gates = _inproj(
        x, meta_chunk, norm_mix, w[:, :o_z].astype(BF16), w[:, o_z:o_xbc].astype(BF16),
        w[:, o_xbc:o_dt].astype(BF16), w_dt.astype(BF16), w_dt.T.astype(BF16), w[:, o_g:].astype(BF16),
        dt_bias[None, :], dt_bias[:, None], ssm_conv_w[0], ssm_conv_b, tm=ROW_TILE)
    y = _ssd(xbc, dt, dtt, a_log, jnp.repeat(ssm_d[0], SSM_HEADDIM)[None, :],
             n_x_chunks=seq // CHUNK, d_inner=d_inner)
    tail_weights = [conv_dw_w[0], conv_dw_b, conv_ln_g, conv_ln_b, conv_out_w[0].astype(BF16), ssm_norm_w,
                    ssm_out_w[0].astype(BF16), w_o[0].astype(BF16), norm_ffn, w_gate[0].astype(BF16),
                    w_up[0].astype(BF16), w_down[0].astype(BF16), norm_final[None, :]]
    return _tail(x, y, z, gates, a, tail_weights, tm=ROW_TILE)
```

```python
import functools

import jax
import jax.numpy as jnp
import numpy as np
from jax import lax
from jax.experimental import pallas as pl
from jax.experimental.pallas import tpu as pltpu

F32 = jnp.float32
BF16 = jnp.bfloat16

N_META = 16
CHUNK = 128
META_PAD = CHUNK - N_META
CONF_KERNEL = 31
SSM_CONV = 7
SSM_HEADDIM = 64
SSM_GROUPS = 4
D_STATE = 128
EPS = 1e-6
LANES = 128
MXU_COLS = 2 * LANES
HALO = 16
ROW_TILE = 2 * CHUNK
VMEM_LIMIT = 56 * 1024 * 1024


def _sigmoid(v):
    return 1.0 / (1.0 + jnp.exp(-v))


def _silu(v):
    return v * _sigmoid(v)


def _softplus(v):
    return jnp.maximum(v, 0.0) + jnp.log1p(jnp.exp(-jnp.abs(v)))


def _dot(a, b):
    return jnp.dot(a, b, preferred_element_type=F32)


def _dot_exact(a, b):
    return jnp.dot(a, b, preferred_element_type=F32, precision=lax.Precision.HIGHEST)


def _resident(shape):
    nd = len(shape)
    return pl.BlockSpec(shape, lambda *_: (0,) * nd, pipeline_mode=pl.Buffered(1))


def _dwconv_tile(buf_ref, w_ref, b_ref, ksize, t, rows, row0=0):
    half = (ksize - 1) // 2
    sl = slice(t * LANES, (t + 1) * LANES)
    acc = jnp.broadcast_to(b_ref[:, sl], (rows, LANES))
    for k in range(ksize):
        start = row0 + HALO - half + k
        acc = acc + buf_ref[t, start:start + rows, :] * w_ref[k:k + 1, sl]
    return acc


def _order_after(dst_ref, src):
    part = None
    for r in range(0, src.shape[0], HALO):
        for c in range(0, src.shape[1], LANES):
            piece = src[r:r + HALO, c:c + LANES]
            part = piece if part is None else part + piece
    zero = pltpu.bitcast((pltpu.bitcast(part, jnp.uint32) >> 16) >> 16, F32)
    dst_ref[0:HALO, 0:LANES] = dst_ref[0:HALO, 0:LANES] + zero.astype(dst_ref.dtype)


def _inproj_kernel(prev_ref, main_ref, next_ref, meta_ref, g_ref, w_ref, w_gates_ref, w_dtt_ref, dtb_ref, dtbt_ref,
                   cw_ref, cb_ref, a_ref, z_ref, xbc_ref, dt_ref, dtt_ref, gates_ref, hs_ref, hn_ref, buf_ref,
                   *, n_x_tiles, conf, d_inner):
    tm = main_ref.shape[1]
    n_xbc = xbc_ref.shape[2]
    n_dt = dt_ref.shape[2]
    o_z = 2 * conf
    o_xbc = o_z + d_inner
    o_dt = o_xbc + n_xbc
    j = pl.program_id(1)

    @pl.when(j < n_x_tiles)
    def _():
        hs_ref[0:HALO, :] = jnp.where(j == 0, meta_ref[META_PAD:CHUNK, :], prev_ref[0])
        hs_ref[HALO:HALO + tm, :] = main_ref[0]
        hs_ref[HALO + tm:, :] = jnp.where(j == n_x_tiles - 1, 0.0, next_ref[0])

    @pl.when(j == n_x_tiles)
    def _():
        hs_ref[0:HALO, :] = jnp.zeros((HALO, hs_ref.shape[1]), F32)
        hs_ref[HALO:HALO + CHUNK, :] = meta_ref[...]
        hs_ref[HALO + CHUNK:HALO + CHUNK + HALO, :] = next_ref[0]
        hs_ref[HALO + CHUNK + HALO:, :] = jnp.zeros((tm - CHUNK, hs_ref.shape[1]), F32)

    x = hs_ref[...]
    ms = jnp.mean(x * x, axis=-1, keepdims=True)
    hn_ref[...] = (x * lax.rsqrt(ms + EPS) * g_ref[...]).astype(BF16)
    rows = slice(HALO, HALO + tm)

    def cols(k, base=0):
        return slice(base + k * MXU_COLS, base + (k + 1) * MXU_COLS)

    def xbc_chunk(k):
        r = _dot(hn_ref[...], w_ref[:, cols(k, o_xbc)])
        buf_ref[2 * k] = r[:, :LANES]
        buf_ref[2 * k + 1] = r[:, LANES:]

    def conv_tile(t):
        xbc_ref[0, :, t * LANES:(t + 1) * LANES] = _silu(
            _dwconv_tile(buf_ref, cw_ref, cb_ref, SSM_CONV, t, tm)).astype(BF16)

    def glu_chunk(k):
        a_ref[0, :, cols(k)] = (_dot(hn_ref[rows, :], w_ref[:, cols(k)])
                                * _sigmoid(_dot(hn_ref[rows, :], w_ref[:, cols(k, conf)]))).astype(BF16)

    def z_chunk(k):
        z_ref[0, :, cols(k)] = _dot(hn_ref[rows, :], w_ref[:, cols(k, o_z)]).astype(BF16)

    def gates_chunk(k):
        gates_ref[0, :, cols(k)] = _sigmoid(_dot(hn_ref[rows, :], w_gates_ref[:, cols(k)])).astype(BF16)

    others = ([functools.partial(glu_chunk, k) for k in range(conf // MXU_COLS)]
              + [functools.partial(z_chunk, k) for k in range(d_inner // MXU_COLS)]
              + [functools.partial(gates_chunk, k) for k in range(gates_ref.shape[2] // MXU_COLS)])
    n_chunks = n_xbc // MXU_COLS
    xbc_chunk(0)
    for k in range(n_chunks):
        if k + 1 < n_chunks:
            xbc_chunk(k + 1)
        share = (len(others) + n_chunks - 1 - k) // (n_chunks - k)
        for i in range(2):
            for _ in range((share + 1 - i) // 2):
                others.pop(0)()
            conv_tile(2 * k + i)

    hn = hn_ref[rows, :]
    dt = _softplus(_dot(hn, w_ref[:, o_dt:o_dt + n_dt]) + dtb_ref[...])
    dtt = _softplus(lax.dot_general(w_dtt_ref[...], hn, (((1,), (1,)), ((), ())),
                                    preferred_element_type=F32) + dtbt_ref[...])
    first_live = jnp.where(j == n_x_tiles, META_PAD, 0)
    dt_ref[0] = jnp.where(lax.broadcasted_iota(jnp.int32, (tm, 1), 0) >= first_live, dt, 0.0)
    dtt_ref[0] = jnp.where(lax.broadcasted_iota(jnp.int32, (1, tm), 1) >= first_live, dtt, 0.0)


def _inproj(x, meta_chunk, g, w, w_gates, w_dtt, dtb, dtbt, cw, cb, *, tm, conf, d_inner):
    bsz, seq, d = x.shape
    frame = seq + CHUNK
    n_x_tiles = seq // tm
    per = tm // HALO
    n_xbc, n_dt, n_g = cw.shape[1], w_dtt.shape[0], w_gates.shape[1]
    last = n_x_tiles - 1
    main_map = lambda b, j: (b, jnp.minimum(j, last), 0)
    prev_map = lambda b, j: (b, jnp.where(jnp.logical_or(j == 0, j > last), 0, j * per - 1), 0)
    next_map = lambda b, j: (b, jnp.where(j >= last, 0, (j + 1) * per), 0)
    out_spec = lambda n: pl.BlockSpec((1, tm, n), lambda b, j: (b, j, 0))
    weights = [g, w, w_gates, w_dtt, dtb, dtbt, cw, cb]
    return pl.pallas_call(
        functools.partial(_inproj_kernel, n_x_tiles=n_x_tiles, conf=conf, d_inner=d_inner),
        grid=(bsz, n_x_tiles + 1),
        in_specs=[pl.BlockSpec((1, HALO, d), prev_map), pl.BlockSpec((1, tm, d), main_map),
                  pl.BlockSpec((1, HALO, d), next_map), _resident(meta_chunk.shape)]
                 + [_resident(v.shape) for v in weights],
        out_specs=[out_spec(conf), out_spec(d_inner), out_spec(n_xbc), out_spec(n_dt),
                   pl.BlockSpec((1, n_dt, tm), lambda b, j: (b, 0, j)), out_spec(n_g)],
        out_shape=[jax.ShapeDtypeStruct((bsz, frame, conf), BF16), jax.ShapeDtypeStruct((bsz, frame, d_inner), BF16),
                   jax.ShapeDtypeStruct((bsz, frame, n_xbc), BF16), jax.ShapeDtypeStruct((bsz, frame, n_dt), F32),
                   jax.ShapeDtypeStruct((bsz, n_dt, frame), F32), jax.ShapeDtypeStruct((bsz, frame, n_g), BF16)],
        scratch_shapes=[pltpu.VMEM((tm + 2 * HALO, d), F32), pltpu.VMEM((tm + 2 * HALO, d), BF16),
                        pltpu.VMEM((n_xbc // LANES, tm + 2 * HALO, LANES), F32)],
        compiler_params=pltpu.CompilerParams(dimension_semantics=("parallel", "parallel"),
                                             vmem_limit_bytes=VMEM_LIMIT),
        name="inproj",
    )(x, x, x, meta_chunk, *weights)


def _spread_matrix(n):
    return np.kron(np.eye(n, dtype=np.float32), np.ones((1, SSM_HEADDIM), np.float32))


def _triangles():
    li = lax.broadcasted_iota(jnp.int32, (CHUNK, CHUNK), 0)
    si = lax.broadcasted_iota(jnp.int32, (CHUNK, CHUNK), 1)
    return (li >= si).astype(F32), (li <= si).astype(F32), li, si


def _hi_lo_rows(v):
    hi = v.astype(BF16).astype(F32)
    row = lax.broadcasted_iota(jnp.int32, (16, v.shape[1]), 0)
    return jnp.where(row == 0, hi, jnp.where(row == 1, v - hi, 0.0)).astype(BF16)


def _ssd_bwd_state_kernel(x_ref, dt_ref, alog_row_ref, spread_ref, sb_ref, state_ref, *, d_inner, n_heads, hpg):
    @pl.when(pl.program_id(1) == 0)
    def _():
        state_ref[...] = jnp.zeros_like(state_ref)

    gp = hpg * SSM_HEADDIM
    _, upper, _, _ = _triangles()
    dt_c = dt_ref[0, :, n_heads:]
    cum = _dot_exact(upper, dt_c * -jnp.exp(alog_row_ref[:, n_heads:]))
    tot = cum[0:1, :]
    to_edge = jnp.exp(tot - cum) * dt_c
    wide = _dot(jnp.concatenate([to_edge.astype(BF16), _hi_lo_rows(jnp.exp(tot))], axis=0), spread_ref[...])
    for g in range(SSM_GROUPS):
        lanes = slice(g * gp, (g + 1) * gp)
        bm = x_ref[0, :, d_inner + g * D_STATE:d_inner + (g + 1) * D_STATE]
        weighted = (x_ref[0, :, lanes].astype(F32) * wide[:CHUNK, lanes]).astype(BF16)
        s_local = lax.dot_general(bm, weighted, (((0,), (0,)), ((), ())), preferred_element_type=F32)
        s_in = state_ref[g]
        sb_ref[0, 0, g * D_STATE:(g + 1) * D_STATE, :] = s_in.astype(sb_ref.dtype)
        state_ref[g] = (wide[CHUNK:CHUNK + 1, lanes] + wide[CHUNK + 1:CHUNK + 2, lanes]) * s_in + s_local


def _ssd_main_kernel(x_ref, dt_ref, dtt_ref, sb_ref, alog_row_ref, alog_col_ref, dskip_ref, spread_ref,
                     y_ref, sf_ref, *, d_inner, n_heads, hpg):
    @pl.when(pl.program_id(1) == 0)
    def _():
        sf_ref[...] = jnp.zeros_like(sf_ref)

    gp = hpg * SSM_HEADDIM
    lower, upper, li, si = _triangles()
    fwd_sees = li >= si
    on_diag = li == si
    left_head = lax.broadcasted_iota(jnp.int32, (CHUNK, 2 * SSM_HEADDIM), 1) < SSM_HEADDIM

    dt_c = dt_ref[0]
    dt_r = dtt_ref[0]
    adt_c = dt_c * -jnp.exp(alog_row_ref[...])
    adt_r = dt_r * -jnp.exp(alog_col_ref[...])
    cumf_c = _dot_exact(lower, adt_c[:, :n_heads])
    cumb_c = _dot_exact(upper, adt_c[:, n_heads:])
    rowf = _dot_exact(adt_r[:n_heads], upper) - jnp.log(dt_r[:n_heads])
    rowb = _dot_exact(adt_r[n_heads:], lower) - jnp.log(dt_r[n_heads:])
    tot_f = cumf_c[CHUNK - 1:CHUNK, :]
    to_edge_f = jnp.exp(tot_f - cumf_c) * dt_c[:, :n_heads]

    cbs, diag_cols = [], []
    for g in range(SSM_GROUPS):
        bm = x_ref[0, :, d_inner + g * D_STATE:d_inner + (g + 1) * D_STATE]
        cm = x_ref[0, :, d_inner + (SSM_GROUPS + g) * D_STATE:d_inner + (SSM_GROUPS + g + 1) * D_STATE]
        cb = lax.dot_general(cm, bm, (((1,), (1,)), ((), ())), preferred_element_type=F32)
        cbs.append(cb)
        cb_diag = jnp.sum(jnp.where(on_diag, cb, 0.0), axis=1, keepdims=True)
        diag_cols.append(cb_diag * dt_c[:, n_heads + g * hpg:n_heads + (g + 1) * hpg])
    cols = jnp.concatenate([to_edge_f, jnp.exp(cumf_c), jnp.exp(cumb_c)] + diag_cols, axis=1)
    zeros = jnp.zeros((1, n_heads), F32)
    extra = _hi_lo_rows(jnp.concatenate([zeros, jnp.exp(tot_f), zeros, zeros], axis=1))
    wide = _dot(jnp.concatenate([cols.astype(BF16), extra], axis=0), spread_ref[...])

    for g in range(SSM_GROUPS):
        lanes = lambda q: slice(q * d_inner + g * gp, q * d_inner + (g + 1) * gp)
        bm = x_ref[0, :, d_inner + g * D_STATE:d_inner + (g + 1) * D_STATE]
        cm = x_ref[0, :, d_inner + (SSM_GROUPS + g) * D_STATE:d_inner + (SSM_GROUPS + g + 1) * D_STATE]
        xf = x_ref[0, :, g * gp:(g + 1) * gp].astype(F32)
        s_in = sf_ref[g]
        y_off = _dot(cm, jnp.concatenate([s_in.astype(BF16), sb_ref[0, 0, g * D_STATE:(g + 1) * D_STATE, :]], axis=1))
        weighted = (xf * wide[:CHUNK, lanes(0)]).astype(BF16)
        s_local = lax.dot_general(bm, weighted, (((0,), (0,)), ((), ())), preferred_element_type=F32)
        sf_ref[g] = (wide[CHUNK:CHUNK + 1, lanes(1)] + wide[CHUNK + 1:CHUNK + 2, lanes(1)]) * s_in + s_local
        y = (y_off[:, :gp] * wide[:CHUNK, lanes(1)] + y_off[:, gp:] * wide[:CHUNK, lanes(2)]
             + xf * (dskip_ref[:, g * gp:(g + 1) * gp] + wide[:CHUNK, lanes(3)]))

        pairs = []
        for j in range(hpg // 2):
            ms = []
            for h in (g * hpg + 2 * j, g * hpg + 2 * j + 1):
                df = jnp.broadcast_to(cumf_c[:, h:h + 1], (CHUNK, CHUNK)) - jnp.broadcast_to(rowf[h:h + 1, :], (CHUNK, CHUNK))
                db = jnp.broadcast_to(cumb_c[:, h:h + 1], (CHUNK, CHUNK)) - jnp.broadcast_to(rowb[h:h + 1, :], (CHUNK, CHUNK))
                ms.append((cbs[g] * jnp.exp(jnp.where(fwd_sees, df, db))).astype(BF16))
            xp = x_ref[0, :, g * gp + 2 * j * SSM_HEADDIM:g * gp + (2 * j + 2) * SSM_HEADDIM]
            zero = jnp.zeros_like(xp)
            block_diag = jnp.concatenate([jnp.where(left_head, xp, zero), jnp.where(left_head, zero, xp)], axis=0)
            pairs.append(_dot(jnp.concatenate(ms, axis=1), block_diag))
        y_ref[0, :, g * gp:(g + 1) * gp] = (y + jnp.concatenate(pairs, axis=1)).astype(y_ref.dtype)


def _ssd(xbc, dt, dtt, a_log, dskip, *, n_x_chunks, d_inner):
    bsz, frame, c = xbc.shape
    n_heads = dt.shape[2] // 2
    hpg = n_heads // SSM_GROUPS
    gp = hpg * SSM_HEADDIM
    nc = n_x_chunks + 1
    fwd = lambda i: (i + n_x_chunks) % nc
    bwd = lambda i: jnp.where(i == n_x_chunks, n_x_chunks, n_x_chunks - 1 - i)
    row_block = lambda n, order: pl.BlockSpec((1, CHUNK, n), lambda b, i: (b, order(i), 0))
    state_block = lambda order: pl.BlockSpec((1, 1, SSM_GROUPS * D_STATE, gp), lambda b, i: (b, order(i), 0, 0))
    params = pltpu.CompilerParams(dimension_semantics=("parallel", "arbitrary"), vmem_limit_bytes=VMEM_LIMIT)
    state_scratch = pltpu.VMEM((SSM_GROUPS, D_STATE, gp), F32)
    alog_row, alog_col = a_log[None, :], a_log[:, None]
    spread1 = jnp.asarray(_spread_matrix(n_heads), BF16)
    spread4 = jnp.asarray(_spread_matrix(4 * n_heads), BF16)

    sb = pl.pallas_call(
        functools.partial(_ssd_bwd_state_kernel, d_inner=d_inner, n_heads=n_heads, hpg=hpg),
        grid=(bsz, nc),
        in_specs=[row_block(c, bwd), row_block(2 * n_heads, bwd), _resident(alog_row.shape),
                  _resident(spread1.shape)],
        out_specs=state_block(bwd),
        out_shape=jax.ShapeDtypeStruct((bsz, nc, SSM_GROUPS * D_STATE, gp), BF16),
        scratch_shapes=[state_scratch],
        compiler_params=params,
        name="ssd_bwd_state",
    )(xbc, dt, alog_row, spread1)

    return pl.pallas_call(
        functools.partial(_ssd_main_kernel, d_inner=d_inner, n_heads=n_heads, hpg=hpg),
        grid=(bsz, nc),
        in_specs=[row_block(c, fwd), row_block(2 * n_heads, fwd),
                  pl.BlockSpec((1, 2 * n_heads, CHUNK), lambda b, i: (b, 0, fwd(i))), state_block(fwd),
                  _resident(alog_row.shape), _resident(alog_col.shape), _resident(dskip.shape),
                  _resident(spread4.shape)],
        out_specs=row_block(d_inner, fwd),
        out_shape=jax.ShapeDtypeStruct((bsz, frame, d_inner), BF16),
        scratch_shapes=[state_scratch],
        compiler_params=params,
        name="ssd_main",
    )(xbc, dt, dtt, sb, alog_row, alog_col, dskip, spread4)


def _tail_kernel(x_ref, y_ref, z_ref, gates_ref, a0_prev_ref, a0_ref, a0_next_ref, a_prev_ref, a_ref, a_next_ref,
                 cw_ref, cb_ref, lng_ref, lnb_ref, wco_ref, nw_ref, wso_ref, wo_ref, nffn_ref,
                 wg_ref, wu_ref, wd_ref, nfin_ref, o_ref, buf_ref, conv_ref, h_ref, act_ref, hn_ref):
    tm, d = x_ref.shape[1], x_ref.shape[2]
    d_inner = y_ref.shape[2]
    gc = d_inner // SSM_GROUPS
    n_tiles = buf_ref.shape[0]

    def fill_window(p_ref, m_ref, n_ref):
        for t in range(n_tiles):
            sl = slice(t * LANES, (t + 1) * LANES)
            buf_ref[t, 0:HALO, :] = p_ref[0, :, sl].astype(F32)
            buf_ref[t, HALO:HALO + tm, :] = m_ref[0, :, sl].astype(F32)
            buf_ref[t, HALO + tm:, :] = n_ref[0, :, sl].astype(F32)

    n_units = n_tiles

    def conv_unit(t):
        acc = _dwconv_tile(buf_ref, cw_ref, cb_ref, CONF_KERNEL, t, tm)
        conv_ref[:, t * LANES:(t + 1) * LANES] = acc
        return acc

    def norm_act():
        v = conv_ref[...]
        mu = jnp.mean(v, axis=-1, keepdims=True)
        vc = v - mu
        var = jnp.mean(vc * vc, axis=-1, keepdims=True)
        h = _silu(vc * lax.rsqrt(var + EPS) * lng_ref[...] + lnb_ref[...])
        h_ref[...] = h.astype(BF16)
        return h

    @pl.when(pl.program_id(1) == 0)
    def _():
        fill_window(a0_prev_ref, a0_ref, a0_next_ref)
        for u in range(n_units):
            conv_unit(u)
        norm_act()

    y_conv = _dot(h_ref[...], wco_ref[...])
    fill_window(a_prev_ref, a_ref, a_next_ref)

    v = y_ref[0].astype(F32) * _silu(z_ref[0].astype(F32))
    parts = []
    for g in range(SSM_GROUPS):
        vg = v[:, g * gc:(g + 1) * gc]
        ms = jnp.mean(vg * vg, axis=-1, keepdims=True)
        parts.append((vg * lax.rsqrt(ms + EPS) * nw_ref[:, g * gc:(g + 1) * gc]).astype(BF16))
    y_ssm = _dot(jnp.concatenate(parts, axis=1), wso_ref[...])

    merged = gates_ref[0, :, :d].astype(F32) * y_conv + gates_ref[0, :, d:].astype(F32) * y_ssm
    hs = x_ref[0] + _dot(merged.astype(BF16), wo_ref[...])

    ms = jnp.mean(hs * hs, axis=-1, keepdims=True)
    hn_ref[...] = (hs * lax.rsqrt(ms + EPS) * nffn_ref[...]).astype(BF16)
    n_ffn_chunks = wg_ref.shape[1] // MXU_COLS
    for c in range(n_ffn_chunks):
        sl = slice(c * MXU_COLS, (c + 1) * MXU_COLS)
        hn = hn_ref[...]
        act_ref[:, sl] = (_silu(_dot(hn, wg_ref[:, sl])) * _dot(hn, wu_ref[:, sl])).astype(BF16)
        if c < n_units:
            _order_after(hn_ref, conv_unit(c))
    hs = hs + _dot(act_ref[...], wd_ref[...])
    norm_act()

    ms = jnp.mean(hs * hs, axis=-1, keepdims=True)
    o_ref[0] = hs * lax.rsqrt(ms + EPS) * nfin_ref[...]


def _tail(x, y, z, gates, a, weights, *, tm):
    bsz, seq, d = x.shape
    d_inner = y.shape[2]
    conf = a.shape[2]
    d_ff = weights[-4].shape[1]
    per = tm // HALO
    last = seq // tm - 1
    meta_rows_block = (seq + META_PAD) // HALO
    blk = lambda n: pl.BlockSpec((1, tm, n), lambda b, j: (b, j, 0))
    halo = lambda index: pl.BlockSpec((1, HALO, conf), lambda b, j: (b, index(j), 0))
    nxt = lambda j: jnp.minimum(j + 1, last)
    return pl.pallas_call(
        _tail_kernel,
        grid=(bsz, seq // tm),
        in_specs=[blk(d), blk(d_inner), blk(d_inner), blk(2 * d),
                  halo(lambda j: meta_rows_block), pl.BlockSpec((1, tm, conf), lambda b, j: (b, 0, 0)),
                  halo(lambda j: per),
                  halo(lambda j: (j + 1) * per - 1), pl.BlockSpec((1, tm, conf), lambda b, j: (b, nxt(j), 0)),
                  halo(lambda j: (nxt(j) + 1) * per)]
                 + [_resident(w.shape) for w in weights],
        out_specs=blk(d),
        out_shape=jax.ShapeDtypeStruct((bsz, seq, d), F32),
        scratch_shapes=[pltpu.VMEM((conf // LANES, tm + 2 * HALO, LANES), F32), pltpu.VMEM((tm, conf), F32),
                        pltpu.VMEM((tm, conf), BF16), pltpu.VMEM((tm, d_ff), BF16), pltpu.VMEM((tm, d), BF16)],
        compiler_params=pltpu.CompilerParams(dimension_semantics=("parallel", "arbitrary"),
                                             vmem_limit_bytes=VMEM_LIMIT),
        name="tail",
    )(x, y, z, gates, a, a, a, a, a, a, *weights)


def kernel(x, meta_tokens, norm_mix, w_in, conv_dw_w, conv_dw_b, conv_ln_g, conv_ln_b, conv_out_w,
           ssm_conv_w, ssm_conv_b, dt_bias_f, dt_bias_b, a_log_f, a_log_b, ssm_d, ssm_norm_w, ssm_out_w,
           w_o, norm_ffn, w_gate, w_up, w_down, norm_final):
    bsz, seq, d = x.shape
    assert norm_mix.shape[0] == 1 and seq % ROW_TILE == 0 and meta_tokens.shape[0] == N_META
    conf = conv_dw_w.shape[2]
    d_inner = ssm_norm_w.shape[1]
    n_heads = ssm_d.shape[1]
    n_xbc = ssm_conv_w.shape[2]
    assert conf == d

    w = w_in[0].astype(BF16)
    o_dt = 2 * conf + d_inner + n_xbc
    o_g = o_dt + 2 * n_heads
    dt_bias = jnp.concatenate([dt_bias_f[0], dt_bias_b[0]])
    a_log = jnp.concatenate([a_log_f[0], a_log_b[0]])
    meta_chunk = jnp.concatenate([jnp.zeros((META_PAD, d), x.dtype), meta_tokens.astype(x.dtype)], axis=0)

    a, z, xbc, dt, dtt, gates = _inproj(
        x, meta_chunk, norm_mix, w, w[:, o_g:], w[:, o_dt:o_g].T, dt_bias[None, :], dt_bias[:, None],
        ssm_conv_w[0], ssm_conv_b, tm=ROW_TILE, conf=conf, d_inner=d_inner)
    y = _ssd(xbc, dt, dtt, a_log, jnp.repeat(ssm_d[0], SSM_HEADDIM)[None, :],
             n_x_chunks=seq // CHUNK, d_inner=d_inner)
    tail_weights = [conv_dw_w[0], conv_dw_b, conv_ln_g, conv_ln_b, conv_out_w[0].astype(BF16), ssm_norm_w,
                    ssm_out_w[0].astype(BF16), w_o[0].astype(BF16), norm_ffn, w_gate[0].astype(BF16),
                    w_up[0].astype(BF16), w_down[0].astype(BF16), norm_final[None, :]]
    return _tail(x, y, z, gates, a, tail_weights, tm=ROW_TILE)
```

```python
import functools

import jax
import jax.numpy as jnp
import numpy as np
from jax import lax
from jax.experimental import pallas as pl
from jax.experimental.pallas import tpu as pltpu

F32 = jnp.float32
BF16 = jnp.bfloat16

N_META = 16
CHUNK = 128
META_PAD = CHUNK - N_META
CONF_KERNEL = 31
SSM_CONV = 7
SSM_HEADDIM = 64
SSM_GROUPS = 4
D_STATE = 128
EPS = 1e-6
LANES = 128
MXU_COLS = 2 * LANES
HALO = 16
ROW_TILE = 2 * CHUNK
VMEM_LIMIT = 56 * 1024 * 1024


def _sigmoid(v):
    return 1.0 / (1.0 + jnp.exp(-v))


def _silu(v):
    return v * _sigmoid(v)


def _softplus(v):
    return jnp.maximum(v, 0.0) + jnp.log1p(jnp.exp(-jnp.abs(v)))


def _dot(a, b):
    return jnp.dot(a, b, preferred_element_type=F32)


def _split3(v):
    hi = v.astype(BF16)
    rest = v - hi.astype(F32)
    mid = rest.astype(BF16)
    return hi, mid, (rest - mid.astype(F32)).astype(BF16)


def _running_sum_cols(tri, v):
    hi, mid, lo = _split3(v)
    return _dot(tri, hi) + _dot(tri, mid) + _dot(tri, lo)


def _running_sum_rows(v, tri):
    n = v.shape[0]
    r = _dot(jnp.concatenate(_split3(v), axis=0), tri)
    return r[:n] + r[n:2 * n] + r[2 * n:]


def _resident(shape):
    nd = len(shape)
    return pl.BlockSpec(shape, lambda *_: (0,) * nd, pipeline_mode=pl.Buffered(1))


def _dwconv_tile(buf_ref, w_ref, b_ref, ksize, t, rows, row0=0):
    half = (ksize - 1) // 2
    sl = slice(t * LANES, (t + 1) * LANES)
    acc = jnp.broadcast_to(b_ref[:, sl], (rows, LANES))
    for k in range(ksize):
        start = row0 + HALO - half + k
        acc = acc + buf_ref[t, start:start + rows, :] * w_ref[k:k + 1, sl]
    return acc


def _order_after(dst_ref, src):
    part = None
    for r in range(0, src.shape[0], HALO):
        for c in range(0, src.shape[1], LANES):
            piece = src[r:r + HALO, c:c + LANES]
            part = piece if part is None else part + piece
    zero = pltpu.bitcast((pltpu.bitcast(part, jnp.uint32) >> 16) >> 16, F32)
    dst_ref[0:HALO, 0:LANES] = dst_ref[0:HALO, 0:LANES] + zero.astype(dst_ref.dtype)


def _inproj_kernel(prev_ref, main_ref, next_ref, meta_ref, g_ref, w_ref, w_gates_ref, dtb_ref, dtbt_ref,
                   cw_ref, cb_ref, a_ref, z_ref, xbc_ref, dt_ref, dtt_ref, gates_ref, hs_ref, hn_ref, buf_ref,
                   *, n_x_tiles, conf, d_inner):
    tm = main_ref.shape[1]
    n_xbc = xbc_ref.shape[2]
    n_dt = dt_ref.shape[2]
    o_z = 2 * conf
    o_xbc = o_z + d_inner
    o_dt = o_xbc + n_xbc
    j = pl.program_id(1)

    @pl.when(j < n_x_tiles)
    def _():
        hs_ref[0:HALO, :] = jnp.where(j == 0, meta_ref[META_PAD:CHUNK, :], prev_ref[0])
        hs_ref[HALO:HALO + tm, :] = main_ref[0]
        hs_ref[HALO + tm:, :] = jnp.where(j == n_x_tiles - 1, 0.0, next_ref[0])

    @pl.when(j == n_x_tiles)
    def _():
        hs_ref[0:HALO, :] = jnp.zeros((HALO, hs_ref.shape[1]), F32)
        hs_ref[HALO:HALO + CHUNK, :] = meta_ref[...]
        hs_ref[HALO + CHUNK:HALO + CHUNK + HALO, :] = next_ref[0]
        hs_ref[HALO + CHUNK + HALO:, :] = jnp.zeros((tm - CHUNK, hs_ref.shape[1]), F32)

    x = hs_ref[...]
    ms = jnp.mean(x * x, axis=-1, keepdims=True)
    hn_ref[...] = (x * lax.rsqrt(ms + EPS) * g_ref[...]).astype(BF16)
    rows = slice(HALO, HALO + tm)

    def cols(k, base=0):
        return slice(base + k * MXU_COLS, base + (k + 1) * MXU_COLS)

    def xbc_chunk(k):
        r = _dot(hn_ref[...], w_ref[:, cols(k, o_xbc)])
        buf_ref[2 * k] = r[:, :LANES]
        buf_ref[2 * k + 1] = r[:, LANES:]

    def conv_tile(t):
        xbc_ref[0, :, t * LANES:(t + 1) * LANES] = _silu(
            _dwconv_tile(buf_ref, cw_ref, cb_ref, SSM_CONV, t, tm)).astype(BF16)

    def glu_chunk(k):
        a_ref[0, :, cols(k)] = (_dot(hn_ref[rows, :], w_ref[:, cols(k)])
                                * _sigmoid(_dot(hn_ref[rows, :], w_ref[:, cols(k, conf)]))).astype(BF16)

    def z_chunk(k):
        z_ref[0, :, cols(k)] = _dot(hn_ref[rows, :], w_ref[:, cols(k, o_z)]).astype(BF16)

    def gates_chunk(k):
        gates_ref[0, :, cols(k)] = _sigmoid(_dot(hn_ref[rows, :], w_gates_ref[:, cols(k)])).astype(BF16)

    others = ([functools.partial(glu_chunk, k) for k in range(conf // MXU_COLS)]
              + [functools.partial(z_chunk, k) for k in range(d_inner // MXU_COLS)]
              + [functools.partial(gates_chunk, k) for k in range(gates_ref.shape[2] // MXU_COLS)])
    n_chunks = n_xbc // MXU_COLS
    xbc_chunk(0)
    for k in range(n_chunks):
        if k + 1 < n_chunks:
            xbc_chunk(k + 1)
        share = (len(others) + n_chunks - 1 - k) // (n_chunks - k)
        for i in range(2):
            for _ in range((share + 1 - i) // 2):
                others.pop(0)()
            conv_tile(2 * k + i)

    r = _dot(hn_ref[rows, :], w_ref[:, o_dt:o_dt + LANES])
    dt = _softplus(r[:, :n_dt] + dtb_ref[...])
    dtt = _softplus(r.T[:n_dt, :] + dtbt_ref[...])
    first_live = jnp.where(j == n_x_tiles, META_PAD, 0)
    dt_ref[0] = jnp.where(lax.broadcasted_iota(jnp.int32, (tm, 1), 0) >= first_live, dt, 0.0)
    dtt_ref[0] = jnp.where(lax.broadcasted_iota(jnp.int32, (1, tm), 1) >= first_live, dtt, 0.0)


def _inproj(x, meta_chunk, g, w, w_gates, dtb, dtbt, cw, cb, *, tm, conf, d_inner):
    bsz, seq, d = x.shape
    frame = seq + CHUNK
    n_x_tiles = seq // tm
    per = tm // HALO
    n_xbc, n_dt, n_g = cw.shape[1], dtb.shape[1], w_gates.shape[1]
    last = n_x_tiles - 1
    main_map = lambda b, j: (b, jnp.minimum(j, last), 0)
    prev_map = lambda b, j: (b, jnp.where(jnp.logical_or(j == 0, j > last), 0, j * per - 1), 0)
    next_map = lambda b, j: (b, jnp.where(j >= last, 0, (j + 1) * per), 0)
    out_spec = lambda n: pl.BlockSpec((1, tm, n), lambda b, j: (b, j, 0))
    weights = [g, w, w_gates, dtb, dtbt, cw, cb]
    return pl.pallas_call(
        functools.partial(_inproj_kernel, n_x_tiles=n_x_tiles, conf=conf, d_inner=d_inner),
        grid=(bsz, n_x_tiles + 1),
        in_specs=[pl.BlockSpec((1, HALO, d), prev_map), pl.BlockSpec((1, tm, d), main_map),
                  pl.BlockSpec((1, HALO, d), next_map), _resident(meta_chunk.shape)]
                 + [_resident(v.shape) for v in weights],
        out_specs=[out_spec(conf), out_spec(d_inner), out_spec(n_xbc), out_spec(n_dt),
                   pl.BlockSpec((1, n_dt, tm), lambda b, j: (b, 0, j)), out_spec(n_g)],
        out_shape=[jax.ShapeDtypeStruct((bsz, frame, conf), BF16), jax.ShapeDtypeStruct((bsz, frame, d_inner), BF16),
                   jax.ShapeDtypeStruct((bsz, frame, n_xbc), BF16), jax.ShapeDtypeStruct((bsz, frame, n_dt), F32),
                   jax.ShapeDtypeStruct((bsz, n_dt, frame), F32), jax.ShapeDtypeStruct((bsz, frame, n_g), BF16)],
        scratch_shapes=[pltpu.VMEM((tm + 2 * HALO, d), F32), pltpu.VMEM((tm + 2 * HALO, d), BF16),
                        pltpu.VMEM((n_xbc // LANES, tm + 2 * HALO, LANES), F32)],
        compiler_params=pltpu.CompilerParams(dimension_semantics=("parallel", "parallel"),
                                             vmem_limit_bytes=VMEM_LIMIT),
        name="inproj",
    )(x, x, x, meta_chunk, *weights)


def _spread_matrix(n):
    return np.kron(np.eye(n, dtype=np.float32), np.ones((1, SSM_HEADDIM), np.float32))


def _triangles():
    li = lax.broadcasted_iota(jnp.int32, (CHUNK, CHUNK), 0)
    si = lax.broadcasted_iota(jnp.int32, (CHUNK, CHUNK), 1)
    return (li >= si).astype(F32).astype(BF16), (li <= si).astype(F32).astype(BF16), li, si


def _hi_lo_rows(v):
    hi = v.astype(BF16).astype(F32)
    row = lax.broadcasted_iota(jnp.int32, (16, v.shape[1]), 0)
    return jnp.where(row == 0, hi, jnp.where(row == 1, v - hi, 0.0)).astype(BF16)


def _ssd_bwd_state_kernel(x_ref, dt_ref, alog_row_ref, spread_ref, sb_ref, state_ref, *, d_inner, n_heads, hpg):
    @pl.when(pl.program_id(1) == 0)
    def _():
        state_ref[...] = jnp.zeros_like(state_ref)

    gp = hpg * SSM_HEADDIM
    _, upper, _, _ = _triangles()
    neg_a = -jnp.exp(alog_row_ref[:, n_heads:])
    chunks = []
    for c in (1, 0):
        rows = slice(c * CHUNK, (c + 1) * CHUNK)
        dt_c = dt_ref[0, rows, n_heads:]
        cum = _running_sum_cols(upper, dt_c * neg_a)
        tot = cum[0:1, :]
        to_edge = jnp.exp(tot - cum) * dt_c
        chunks.append((c, rows, _dot(jnp.concatenate([to_edge.astype(BF16), _hi_lo_rows(jnp.exp(tot))], axis=0),
                                     spread_ref[...])))
    for g in range(SSM_GROUPS):
        lanes = slice(g * gp, (g + 1) * gp)
        state = state_ref[g]
        for c, rows, wide in chunks:
            bm = x_ref[0, rows, d_inner + g * D_STATE:d_inner + (g + 1) * D_STATE]
            weighted = (x_ref[0, rows, lanes].astype(F32) * wide[:CHUNK, lanes]).astype(BF16)
            s_local = lax.dot_general(bm, weighted, (((0,), (0,)), ((), ())), preferred_element_type=F32)
            sb_ref[0, c, g * D_STATE:(g + 1) * D_STATE, :] = state.astype(sb_ref.dtype)
            state = (wide[CHUNK:CHUNK + 1, lanes] + wide[CHUNK + 1:CHUNK + 2, lanes]) * state + s_local
        state_ref[g] = state


def _ssd_main_kernel(x_ref, dt_ref, dtt_ref, sb_ref, alog_row_ref, alog_col_ref, dskip_ref, spread_ref,
                     y_ref, sf_ref, *, d_inner, n_heads, hpg):
    @pl.when(pl.program_id(1) == 0)
    def _():
        sf_ref[...] = jnp.zeros_like(sf_ref)

    gp = hpg * SSM_HEADDIM
    lower, upper, li, si = _triangles()
    fwd_sees = li >= si
    on_diag = li == si
    left_head = lax.broadcasted_iota(jnp.int32, (CHUNK, 2 * SSM_HEADDIM), 1) < SSM_HEADDIM

    dt_c = dt_ref[0]
    dt_r = dtt_ref[0]
    adt_c = dt_c * -jnp.exp(alog_row_ref[...])
    adt_r = dt_r * -jnp.exp(alog_col_ref[...])
    cumf_c = _running_sum_cols(lower, adt_c[:, :n_heads])
    cumb_c = _running_sum_cols(upper, adt_c[:, n_heads:])
    rowf = _running_sum_rows(adt_r[:n_heads], upper) - jnp.log(dt_r[:n_heads])
    rowb = _running_sum_rows(adt_r[n_heads:], lower) - jnp.log(dt_r[n_heads:])
    tot_f = cumf_c[CHUNK - 1:CHUNK, :]
    to_edge_f = jnp.exp(tot_f - cumf_c) * dt_c[:, :n_heads]

    cbs, diag_cols = [], []
    for g in range(SSM_GROUPS):
        bm = x_ref[0, :, d_inner + g * D_STATE:d_inner + (g + 1) * D_STATE]
        cm = x_ref[0, :, d_inner + (SSM_GROUPS + g) * D_STATE:d_inner + (SSM_GROUPS + g + 1) * D_STATE]
        cb = lax.dot_general(cm, bm, (((1,), (1,)), ((), ())), preferred_element_type=F32)
        cbs.append(cb)
        cb_diag = jnp.sum(jnp.where(on_diag, cb, 0.0), axis=1, keepdims=True)
        diag_cols.append(cb_diag * dt_c[:, n_heads + g * hpg:n_heads + (g + 1) * hpg])
    cols = jnp.concatenate([to_edge_f, jnp.exp(cumf_c), jnp.exp(cumb_c)] + diag_cols, axis=1)
    zeros = jnp.zeros((1, n_heads), F32)
    extra = _hi_lo_rows(jnp.concatenate([zeros, jnp.exp(tot_f), zeros, zeros], axis=1))
    wide = _dot(jnp.concatenate([cols.astype(BF16), extra], axis=0), spread_ref[...])

    for g in range(SSM_GROUPS):
        lanes = lambda q: slice(q * d_inner + g * gp, q * d_inner + (g + 1) * gp)
        bm = x_ref[0, :, d_inner + g * D_STATE:d_inner + (g + 1) * D_STATE]
        cm = x_ref[0, :, d_inner + (SSM_GROUPS + g) * D_STATE:d_inner + (SSM_GROUPS + g + 1) * D_STATE]
        xf = x_ref[0, :, g * gp:(g + 1) * gp].astype(F32)
        s_in = sf_ref[g]
        y_off = _dot(cm, jnp.concatenate([s_in.astype(BF16), sb_ref[0, 0, g * D_STATE:(g + 1) * D_STATE, :]], axis=1))
        weighted = (xf * wide[:CHUNK, lanes(0)]).astype(BF16)
        s_local = lax.dot_general(bm, weighted, (((0,), (0,)), ((), ())), preferred_element_type=F32)
        sf_ref[g] = (wide[CHUNK:CHUNK + 1, lanes(1)] + wide[CHUNK + 1:CHUNK + 2, lanes(1)]) * s_in + s_local
        y = (y_off[:, :gp] * wide[:CHUNK, lanes(1)] + y_off[:, gp:] * wide[:CHUNK, lanes(2)]
             + xf * (dskip_ref[:, g * gp:(g + 1) * gp] + wide[:CHUNK, lanes(3)]))

        pairs = []
        for j in range(hpg // 2):
            ms = []
            for h in (g * hpg + 2 * j, g * hpg + 2 * j + 1):
                df = jnp.broadcast_to(cumf_c[:, h:h + 1], (CHUNK, CHUNK)) - jnp.broadcast_to(rowf[h:h + 1, :], (CHUNK, CHUNK))
                db = jnp.broadcast_to(cumb_c[:, h:h + 1], (CHUNK, CHUNK)) - jnp.broadcast_to(rowb[h:h + 1, :], (CHUNK, CHUNK))
                ms.append((cbs[g] * jnp.exp(jnp.where(fwd_sees, df, db))).astype(BF16))
            xp = x_ref[0, :, g * gp + 2 * j * SSM_HEADDIM:g * gp + (2 * j + 2) * SSM_HEADDIM]
            zero = jnp.zeros_like(xp)
            block_diag = jnp.concatenate([jnp.where(left_head, xp, zero), jnp.where(left_head, zero, xp)], axis=0)
            pairs.append(_dot(jnp.concatenate(ms, axis=1), block_diag))
        y_ref[0, :, g * gp:(g + 1) * gp] = (y + jnp.concatenate(pairs, axis=1)).astype(y_ref.dtype)


def _ssd(xbc, dt, dtt, a_log, dskip, *, n_x_chunks, d_inner):
    bsz, frame, c = xbc.shape
    n_heads = dt.shape[2] // 2
    hpg = n_heads // SSM_GROUPS
    gp = hpg * SSM_HEADDIM
    nc = n_x_chunks + 1
    fwd = lambda i: (i + n_x_chunks) % nc
    n_pairs = n_x_chunks // 2
    pair_block = lambda *shape: pl.BlockSpec((1,) + shape, lambda b, i: (b, n_pairs - 1 - i) + (0,) * (len(shape) - 1))
    row_block = lambda n: pl.BlockSpec((1, CHUNK, n), lambda b, i: (b, fwd(i), 0))
    params = pltpu.CompilerParams(dimension_semantics=("parallel", "arbitrary"), vmem_limit_bytes=VMEM_LIMIT)
    state_scratch = pltpu.VMEM((SSM_GROUPS, D_STATE, gp), F32)
    alog_row, alog_col = a_log[None, :], a_log[:, None]
    spread1 = jnp.asarray(_spread_matrix(n_heads), BF16)
    spread4 = jnp.asarray(_spread_matrix(4 * n_heads), BF16)

    sb = pl.pallas_call(
        functools.partial(_ssd_bwd_state_kernel, d_inner=d_inner, n_heads=n_heads, hpg=hpg),
        grid=(bsz, n_pairs),
        in_specs=[pair_block(2 * CHUNK, c), pair_block(2 * CHUNK, 2 * n_heads), _resident(alog_row.shape),
                  _resident(spread1.shape)],
        out_specs=pair_block(2, SSM_GROUPS * D_STATE, gp),
        out_shape=jax.ShapeDtypeStruct((bsz, n_x_chunks, SSM_GROUPS * D_STATE, gp), BF16),
        scratch_shapes=[state_scratch],
        compiler_params=params,
        name="ssd_bwd_state",
    )(xbc, dt, alog_row, spread1)

    state_block = pl.BlockSpec((1, 1, SSM_GROUPS * D_STATE, gp),
                               lambda b, i: (b, jnp.minimum(fwd(i), n_x_chunks - 1), 0, 0))
    return pl.pallas_call(
        functools.partial(_ssd_main_kernel, d_inner=d_inner, n_heads=n_heads, hpg=hpg),
        grid=(bsz, nc),
        in_specs=[row_block(c), row_block(2 * n_heads),
                  pl.BlockSpec((1, 2 * n_heads, CHUNK), lambda b, i: (b, 0, fwd(i))), state_block,
                  _resident(alog_row.shape), _resident(alog_col.shape), _resident(dskip.shape),
                  _resident(spread4.shape)],
        out_specs=row_block(d_inner),
        out_shape=jax.ShapeDtypeStruct((bsz, frame, d_inner), BF16),
        scratch_shapes=[state_scratch],
        compiler_params=params,
        name="ssd_main",
    )(xbc, dt, dtt, sb, alog_row, alog_col, dskip, spread4)


def _tail_kernel(x_ref, y_ref, z_ref, gates_ref, a0_prev_ref, a0_ref, a0_next_ref, a_prev_ref, a_ref, a_next_ref,
                 cw_ref, cb_ref, lng_ref, lnb_ref, wco_ref, nw_ref, wso_ref, wo_ref, nffn_ref,
                 wg_ref, wu_ref, wd_ref, nfin_ref, o_ref, buf_ref, conv_ref, h_ref, act_ref, hn_ref):
    tm, d = x_ref.shape[1], x_ref.shape[2]
    d_inner = y_ref.shape[2]
    gc = d_inner // SSM_GROUPS
    n_tiles = buf_ref.shape[0]

    def fill_window(p_ref, m_ref, n_ref):
        for t in range(n_tiles):
            sl = slice(t * LANES, (t + 1) * LANES)
            buf_ref[t, 0:HALO, :] = p_ref[0, :, sl].astype(F32)
            buf_ref[t, HALO:HALO + tm, :] = m_ref[0, :, sl].astype(F32)
            buf_ref[t, HALO + tm:, :] = n_ref[0, :, sl].astype(F32)

    n_units = n_tiles

    def conv_unit(t):
        acc = _dwconv_tile(buf_ref, cw_ref, cb_ref, CONF_KERNEL, t, tm)
        conv_ref[:, t * LANES:(t + 1) * LANES] = acc
        return acc

    def norm_act():
        v = conv_ref[...]
        mu = jnp.mean(v, axis=-1, keepdims=True)
        vc = v - mu
        var = jnp.mean(vc * vc, axis=-1, keepdims=True)
        h = _silu(vc * lax.rsqrt(var + EPS) * lng_ref[...] + lnb_ref[...])
        h_ref[...] = h.astype(BF16)
        return h

    @pl.when(pl.program_id(1) == 0)
    def _():
        fill_window(a0_prev_ref, a0_ref, a0_next_ref)
        for u in range(n_units):
            conv_unit(u)
        norm_act()

    y_conv = _dot(h_ref[...], wco_ref[...])
    fill_window(a_prev_ref, a_ref, a_next_ref)

    v = y_ref[0].astype(F32) * _silu(z_ref[0].astype(F32))
    parts = []
    for g in range(SSM_GROUPS):
        vg = v[:, g * gc:(g + 1) * gc]
        ms = jnp.mean(vg * vg, axis=-1, keepdims=True)
        parts.append((vg * lax.rsqrt(ms + EPS) * nw_ref[:, g * gc:(g + 1) * gc]).astype(BF16))
    y_ssm = _dot(jnp.concatenate(parts, axis=1), wso_ref[...])

    merged = gates_ref[0, :, :d].astype(F32) * y_conv + gates_ref[0, :, d:].astype(F32) * y_ssm
    hs = x_ref[0] + _dot(merged.astype(BF16), wo_ref[...])

    ms = jnp.mean(hs * hs, axis=-1, keepdims=True)
    hn_ref[...] = (hs * lax.rsqrt(ms + EPS) * nffn_ref[...]).astype(BF16)
    n_ffn_chunks = wg_ref.shape[1] // MXU_COLS
    for c in range(n_ffn_chunks):
        sl = slice(c * MXU_COLS, (c + 1) * MXU_COLS)
        hn = hn_ref[...]
        act_ref[:, sl] = (_silu(_dot(hn, wg_ref[:, sl])) * _dot(hn, wu_ref[:, sl])).astype(BF16)
        if c < n_units:
            _order_after(hn_ref, conv_unit(c))
    hs = hs + _dot(act_ref[...], wd_ref[...])
    norm_act()

    ms = jnp.mean(hs * hs, axis=-1, keepdims=True)
    o_ref[0] = hs * lax.rsqrt(ms + EPS) * nfin_ref[...]


def _tail(x, y, z, gates, a, weights, *, tm):
    bsz, seq, d = x.shape
    d_inner = y.shape[2]
    conf = a.shape[2]
    d_ff = weights[-4].shape[1]
    per = tm // HALO
    last = seq // tm - 1
    meta_rows_block = (seq + META_PAD) // HALO
    blk = lambda n: pl.BlockSpec((1, tm, n), lambda b, j: (b, j, 0))
    halo = lambda index: pl.BlockSpec((1, HALO, conf), lambda b, j: (b, index(j), 0))
    nxt = lambda j: jnp.minimum(j + 1, last)
    return pl.pallas_call(
        _tail_kernel,
        grid=(bsz, seq // tm),
        in_specs=[blk(d), blk(d_inner), blk(d_inner), blk(2 * d),
                  halo(lambda j: meta_rows_block), pl.BlockSpec((1, tm, conf), lambda b, j: (b, 0, 0)),
                  halo(lambda j: per),
                  halo(lambda j: (j + 1) * per - 1), pl.BlockSpec((1, tm, conf), lambda b, j: (b, nxt(j), 0)),
                  halo(lambda j: (nxt(j) + 1) * per)]
                 + [_resident(w.shape) for w in weights],
        out_specs=blk(d),
        out_shape=jax.ShapeDtypeStruct((bsz, seq, d), F32),
        scratch_shapes=[pltpu.VMEM((conf // LANES, tm + 2 * HALO, LANES), F32), pltpu.VMEM((tm, conf), F32),
                        pltpu.VMEM((tm, conf), BF16), pltpu.VMEM((tm, d_ff), BF16), pltpu.VMEM((tm, d), BF16)],
        compiler_params=pltpu.CompilerParams(dimension_semantics=("parallel", "arbitrary"),
                                             vmem_limit_bytes=VMEM_LIMIT),
        name="tail",
    )(x, y, z, gates, a, a, a, a, a, a, *weights)


def kernel(x, meta_tokens, norm_mix, w_in, conv_dw_w, conv_dw_b, conv_ln_g, conv_ln_b, conv_out_w,
           ssm_conv_w, ssm_conv_b, dt_bias_f, dt_bias_b, a_log_f, a_log_b, ssm_d, ssm_norm_w, ssm_out_w,
           w_o, norm_ffn, w_gate, w_up, w_down, norm_final):
    bsz, seq, d = x.shape
    assert norm_mix.shape[0] == 1 and seq % ROW_TILE == 0 and meta_tokens.shape[0] == N_META
    conf = conv_dw_w.shape[2]
    d_inner = ssm_norm_w.shape[1]
    n_heads = ssm_d.shape[1]
    n_xbc = ssm_conv_w.shape[2]
    assert conf == d

    w = w_in[0].astype(BF16)
    o_dt = 2 * conf + d_inner + n_xbc
    o_g = o_dt + 2 * n_heads
    dt_bias = jnp.concatenate([dt_bias_f[0], dt_bias_b[0]])
    a_log = jnp.concatenate([a_log_f[0], a_log_b[0]])
    meta_chunk = jnp.concatenate([jnp.zeros((META_PAD, d), x.dtype), meta_tokens.astype(x.dtype)], axis=0)

    a, z, xbc, dt, dtt, gates = _inproj(
        x, meta_chunk, norm_mix, w, w[:, o_g:], dt_bias[None, :], dt_bias[:, None],
        ssm_conv_w[0], ssm_conv_b, tm=ROW_TILE, conf=conf, d_inner=d_inner)
    y = _ssd(xbc, dt, dtt, a_log, jnp.repeat(ssm_d[0], SSM_HEADDIM)[None, :],
             n_x_chunks=seq // CHUNK, d_inner=d_inner)
    tail_weights = [conv_dw_w[0], conv_dw_b, conv_ln_g, conv_ln_b, conv_out_w[0].astype(BF16), ssm_norm_w,
                    ssm_out_w[0].astype(BF16), w_o[0].astype(BF16), norm_ffn, w_gate[0].astype(BF16),
                    w_up[0].astype(BF16), w_down[0].astype(BF16), norm_final[None, :]]
    return _tail(x, y, z, gates, a, tail_weights, tm=ROW_TILE)
```

```python
import functools

import jax
import jax.numpy as jnp
import numpy as np
from jax import lax
from jax.experimental import pallas as pl
from jax.experimental.pallas import tpu as pltpu

F32 = jnp.float32
BF16 = jnp.bfloat16

N_META = 16
CHUNK = 128
META_PAD = CHUNK - N_META
CONF_KERNEL = 31
SSM_CONV = 7
SSM_HEADDIM = 64
SSM_GROUPS = 4
D_STATE = 128
EPS = 1e-6
LANES = 128
MXU_COLS = 2 * LANES
HALO = 16
ROW_TILE = 2 * CHUNK
VMEM_LIMIT = 56 * 1024 * 1024


def _sigmoid(v):
    return 1.0 / (1.0 + jnp.exp(-v))


def _silu(v):
    return v * _sigmoid(v)


def _softplus(v):
    return jnp.maximum(v, 0.0) + jnp.log1p(jnp.exp(-jnp.abs(v)))


def _dot(a, b):
    return jnp.dot(a, b, preferred_element_type=F32)


def _split3(v):
    hi = v.astype(BF16)
    rest = v - hi.astype(F32)
    mid = rest.astype(BF16)
    return hi, mid, (rest - mid.astype(F32)).astype(BF16)


def _running_sum_cols(tri, v):
    hi, mid, lo = _split3(v)
    return _dot(tri, hi) + _dot(tri, mid) + _dot(tri, lo)


def _running_sum_rows(v, tri):
    n = v.shape[0]
    r = _dot(jnp.concatenate(_split3(v), axis=0), tri)
    return r[:n] + r[n:2 * n] + r[2 * n:]


def _resident(shape):
    nd = len(shape)
    return pl.BlockSpec(shape, lambda *_: (0,) * nd, pipeline_mode=pl.Buffered(1))


def _dwconv_tile(buf_ref, w_ref, b_ref, ksize, t, rows, row0=0):
    half = (ksize - 1) // 2
    sl = slice(t * LANES, (t + 1) * LANES)
    acc = jnp.broadcast_to(b_ref[:, sl], (rows, LANES))
    for k in range(ksize):
        start = row0 + HALO - half + k
        acc = acc + buf_ref[t, start:start + rows, :] * w_ref[k:k + 1, sl]
    return acc


def _order_after(dst_ref, src):
    part = None
    for r in range(0, src.shape[0], HALO):
        for c in range(0, src.shape[1], LANES):
            piece = src[r:r + HALO, c:c + LANES]
            part = piece if part is None else part + piece
    zero = pltpu.bitcast((pltpu.bitcast(part, jnp.uint32) >> 16) >> 16, F32)
    dst_ref[0:HALO, 0:LANES] = dst_ref[0:HALO, 0:LANES] + zero.astype(dst_ref.dtype)


def _inproj_kernel(prev_ref, main_ref, next_ref, meta_ref, g_ref, w_ref, w_gates_ref, dtb_ref, dtbt_ref,
                   alog_row_ref, alog_col_ref, cw_ref, cb_ref,
                   a_ref, z_ref, xbc_ref, decc_ref, decr_ref, cols_ref, gates_ref, hs_ref, hn_ref, buf_ref,
                   *, n_x_tiles, conf, d_inner):
    tm = main_ref.shape[1]
    n_xbc = xbc_ref.shape[2]
    n_dt = dtb_ref.shape[1]
    o_z = 2 * conf
    o_xbc = o_z + d_inner
    o_dt = o_xbc + n_xbc
    j = pl.program_id(1)

    @pl.when(j < n_x_tiles)
    def _():
        hs_ref[0:HALO, :] = jnp.where(j == 0, meta_ref[META_PAD:CHUNK, :], prev_ref[0])
        hs_ref[HALO:HALO + tm, :] = main_ref[0]
        hs_ref[HALO + tm:, :] = jnp.where(j == n_x_tiles - 1, 0.0, next_ref[0])

    @pl.when(j == n_x_tiles)
    def _():
        hs_ref[0:HALO, :] = jnp.zeros((HALO, hs_ref.shape[1]), F32)
        hs_ref[HALO:HALO + CHUNK, :] = meta_ref[...]
        hs_ref[HALO + CHUNK:HALO + CHUNK + HALO, :] = next_ref[0]
        hs_ref[HALO + CHUNK + HALO:, :] = jnp.zeros((tm - CHUNK, hs_ref.shape[1]), F32)

    x = hs_ref[...]
    ms = jnp.mean(x * x, axis=-1, keepdims=True)
    hn_ref[...] = (x * lax.rsqrt(ms + EPS) * g_ref[...]).astype(BF16)
    rows = slice(HALO, HALO + tm)

    def cols(k, base=0):
        return slice(base + k * MXU_COLS, base + (k + 1) * MXU_COLS)

    def xbc_chunk(k):
        r = _dot(hn_ref[...], w_ref[:, cols(k, o_xbc)])
        buf_ref[2 * k] = r[:, :LANES]
        buf_ref[2 * k + 1] = r[:, LANES:]

    def conv_tile(t):
        xbc_ref[0, :, t * LANES:(t + 1) * LANES] = _silu(
            _dwconv_tile(buf_ref, cw_ref, cb_ref, SSM_CONV, t, tm)).astype(BF16)

    def glu_chunk(k):
        a_ref[0, :, cols(k)] = (_dot(hn_ref[rows, :], w_ref[:, cols(k)])
                                * _sigmoid(_dot(hn_ref[rows, :], w_ref[:, cols(k, conf)]))).astype(BF16)

    def z_chunk(k):
        z_ref[0, :, cols(k)] = _dot(hn_ref[rows, :], w_ref[:, cols(k, o_z)]).astype(BF16)

    def gates_chunk(k):
        gates_ref[0, :, cols(k)] = _sigmoid(_dot(hn_ref[rows, :], w_gates_ref[:, cols(k)])).astype(BF16)

    others = ([functools.partial(glu_chunk, k) for k in range(conf // MXU_COLS)]
              + [functools.partial(z_chunk, k) for k in range(d_inner // MXU_COLS)]
              + [functools.partial(gates_chunk, k) for k in range(gates_ref.shape[2] // MXU_COLS)])
    def decay_terms():
        r = _dot(hn_ref[rows, :], w_ref[:, o_dt:o_dt + LANES])
        dt = _softplus(r[:, :n_dt] + dtb_ref[...])
        dtt = _softplus(r.T[:n_dt, :] + dtbt_ref[...])
        first_live = jnp.where(j == n_x_tiles, META_PAD, 0)
        dt = jnp.where(lax.broadcasted_iota(jnp.int32, (tm, 1), 0) >= first_live, dt, 0.0)
        dtt = jnp.where(lax.broadcasted_iota(jnp.int32, (1, tm), 1) >= first_live, dtt, 0.0)

        n_heads = n_dt // 2
        lower, upper, _, _ = _triangles()
        neg_a_row = -jnp.exp(alog_row_ref[...])
        neg_a_col = -jnp.exp(alog_col_ref[...])
        for c in range(tm // CHUNK):
            rs = slice(c * CHUNK, (c + 1) * CHUNK)
            dt_c, dt_r = dt[rs, :], dtt[:, rs]
            adt_c, adt_r = dt_c * neg_a_row, dt_r * neg_a_col
            cumf = _running_sum_cols(lower, adt_c[:, :n_heads])
            cumb = _running_sum_cols(upper, adt_c[:, n_heads:])
            to_edge_f = jnp.exp(cumf[CHUNK - 1:CHUNK, :] - cumf) * dt_c[:, :n_heads]
            to_edge_b = jnp.exp(cumb[0:1, :] - cumb) * dt_c[:, n_heads:]
            decc = jnp.concatenate([cumf, cumb, dt_c[:, n_heads:], jnp.zeros_like(cumf)], axis=1)
            cols = jnp.concatenate([to_edge_f, jnp.exp(cumf), jnp.exp(cumb), to_edge_b], axis=1)
            decr = jnp.concatenate(
                [_running_sum_rows(adt_r[:n_heads], upper) - jnp.log(dt_r[:n_heads]),
                 _running_sum_rows(adt_r[n_heads:], lower) - jnp.log(dt_r[n_heads:])], axis=0)
            decc_ref[0, rs, :] = decc
            cols_ref[0, rs, :] = cols.astype(BF16)
            decr_ref[0, :, rs] = decr

    n_chunks = n_xbc // MXU_COLS
    xbc_chunk(0)
    for k in range(n_chunks):
        if k + 1 < n_chunks:
            xbc_chunk(k + 1)
        share = (len(others) + n_chunks - 1 - k) // (n_chunks - k)
        for i in range(2):
            for _ in range((share + 1 - i) // 2):
                others.pop(0)()
            conv_tile(2 * k + i)
    decay_terms()


def _inproj(x, meta_chunk, g, w, w_gates, dtb, dtbt, alog_row, alog_col, cw, cb, *, tm, conf, d_inner):
    bsz, seq, d = x.shape
    frame = seq + CHUNK
    n_x_tiles = seq // tm
    per = tm // HALO
    n_xbc, n_dt, n_g = cw.shape[1], dtb.shape[1], w_gates.shape[1]
    last = n_x_tiles - 1
    main_map = lambda b, j: (b, jnp.minimum(j, last), 0)
    prev_map = lambda b, j: (b, jnp.where(jnp.logical_or(j == 0, j > last), 0, j * per - 1), 0)
    next_map = lambda b, j: (b, jnp.where(j >= last, 0, (j + 1) * per), 0)
    out_spec = lambda n: pl.BlockSpec((1, tm, n), lambda b, j: (b, j, 0))
    weights = [g, w, w_gates, dtb, dtbt, alog_row, alog_col, cw, cb]
    return pl.pallas_call(
        functools.partial(_inproj_kernel, n_x_tiles=n_x_tiles, conf=conf, d_inner=d_inner),
        grid=(bsz, n_x_tiles + 1),
        in_specs=[pl.BlockSpec((1, HALO, d), prev_map), pl.BlockSpec((1, tm, d), main_map),
                  pl.BlockSpec((1, HALO, d), next_map), _resident(meta_chunk.shape)]
                 + [_resident(v.shape) for v in weights],
        out_specs=[out_spec(conf), out_spec(d_inner), out_spec(n_xbc), out_spec(2 * n_dt),
                   pl.BlockSpec((1, n_dt, tm), lambda b, j: (b, 0, j)), out_spec(2 * n_dt), out_spec(n_g)],
        out_shape=[jax.ShapeDtypeStruct((bsz, frame, conf), BF16), jax.ShapeDtypeStruct((bsz, frame, d_inner), BF16),
                   jax.ShapeDtypeStruct((bsz, frame, n_xbc), BF16), jax.ShapeDtypeStruct((bsz, frame, 2 * n_dt), F32),
                   jax.ShapeDtypeStruct((bsz, n_dt, frame), F32), jax.ShapeDtypeStruct((bsz, frame, 2 * n_dt), BF16),
                   jax.ShapeDtypeStruct((bsz, frame, n_g), BF16)],
        scratch_shapes=[pltpu.VMEM((tm + 2 * HALO, d), F32), pltpu.VMEM((tm + 2 * HALO, d), BF16),
                        pltpu.VMEM((n_xbc // LANES, tm + 2 * HALO, LANES), F32)],
        compiler_params=pltpu.CompilerParams(dimension_semantics=("parallel", "parallel"),
                                             vmem_limit_bytes=VMEM_LIMIT),
        name="inproj",
    )(x, x, x, meta_chunk, *weights)


def _spread_matrix(n):
    return np.kron(np.eye(n, dtype=np.float32), np.ones((1, SSM_HEADDIM), np.float32))


def _triangles():
    li = lax.broadcasted_iota(jnp.int32, (CHUNK, CHUNK), 0)
    si = lax.broadcasted_iota(jnp.int32, (CHUNK, CHUNK), 1)
    return (li >= si).astype(F32).astype(BF16), (li <= si).astype(F32).astype(BF16), li, si


def _hi_lo_rows(v):
    hi = v.astype(BF16).astype(F32)
    row = lax.broadcasted_iota(jnp.int32, (16, v.shape[1]), 0)
    return jnp.where(row == 0, hi, jnp.where(row == 1, v - hi, 0.0)).astype(BF16)


def _exp_lanes(row, lo, hi):
    lane = lax.broadcasted_iota(jnp.int32, row.shape, 1)
    return jnp.exp(jnp.where(jnp.logical_and(lane >= lo, lane < hi), row, 0.0))


def _ssd_bwd_state_kernel(x_ref, b_ref, decc_ref, cols_ref, spread_edge_ref, spread_tot_ref, sb_ref, state_ref,
                          *, n_heads, hpg):
    @pl.when(pl.program_id(1) == 0)
    def _():
        state_ref[...] = jnp.zeros_like(state_ref)

    gp = hpg * SSM_HEADDIM
    chunks = []
    for c in (1, 0):
        rows = slice(c * CHUNK, (c + 1) * CHUNK)
        wide = _dot(cols_ref[0, rows, :], spread_edge_ref[...])
        total = _exp_lanes(decc_ref[0, c * CHUNK:c * CHUNK + 1, :], n_heads, 2 * n_heads)
        chunks.append((c, rows, wide, _dot(_hi_lo_rows(total), spread_tot_ref[...])))
    for g in range(SSM_GROUPS):
        lanes = slice(g * gp, (g + 1) * gp)
        state = state_ref[g]
        for c, rows, wide, tot_wide in chunks:
            weighted = (x_ref[0, rows, lanes].astype(F32) * wide[:, lanes]).astype(BF16)
            s_local = lax.dot_general(b_ref[0, rows, g * D_STATE:(g + 1) * D_STATE], weighted,
                                      (((0,), (0,)), ((), ())), preferred_element_type=F32)
            sb_ref[0, c, g * D_STATE:(g + 1) * D_STATE, :] = state.astype(sb_ref.dtype)
            state = (tot_wide[0:1, lanes] + tot_wide[1:2, lanes]) * state + s_local
        state_ref[g] = state


def _ssd_main_kernel(x_ref, decc_ref, decr_ref, cols_ref, sb_ref, dskip_ref, spread3_ref, spread1_ref,
                     y_ref, sf_ref, *, d_inner, n_heads, hpg):
    @pl.when(pl.program_id(1) == 0)
    def _():
        sf_ref[...] = jnp.zeros_like(sf_ref)

    gp = hpg * SSM_HEADDIM
    _, _, li, si = _triangles()
    fwd_sees = li >= si
    on_diag = li == si
    left_head = lax.broadcasted_iota(jnp.int32, (CHUNK, 2 * SSM_HEADDIM), 1) < SSM_HEADDIM

    def b_of(g):
        return x_ref[0, :, d_inner + g * D_STATE:d_inner + (g + 1) * D_STATE]

    def c_of(g):
        return x_ref[0, :, d_inner + (SSM_GROUPS + g) * D_STATE:d_inner + (SSM_GROUPS + g + 1) * D_STATE]

    decc = decc_ref[0]
    decr = decr_ref[0]
    cbs, diag_cols = [], []
    for g in range(SSM_GROUPS):
        cb = lax.dot_general(c_of(g), b_of(g), (((1,), (1,)), ((), ())), preferred_element_type=F32)
        cbs.append(cb)
        cb_diag = jnp.sum(jnp.where(on_diag, cb, 0.0), axis=1, keepdims=True)
        diag_cols.append(cb_diag * decc[:, 2 * n_heads + g * hpg:2 * n_heads + (g + 1) * hpg])
    wide = _dot(cols_ref[0], spread3_ref[...])
    wide_diag = _dot(jnp.concatenate(diag_cols, axis=1).astype(BF16), spread1_ref[...])
    tot_wide = _dot(_hi_lo_rows(_exp_lanes(decc[CHUNK - 1:CHUNK, :], 0, n_heads)), spread3_ref[:, :d_inner])

    for g in range(SSM_GROUPS):
        lanes = slice(g * gp, (g + 1) * gp)
        wlanes = lambda q: slice(q * d_inner + g * gp, q * d_inner + (g + 1) * gp)
        xf = x_ref[0, :, lanes].astype(F32)
        s_in = sf_ref[g]
        y_off = _dot(c_of(g), jnp.concatenate(
            [s_in.astype(BF16), sb_ref[0, 0, g * D_STATE:(g + 1) * D_STATE, :]], axis=1))
        weighted = (xf * wide[:, wlanes(0)]).astype(BF16)
        s_local = lax.dot_general(b_of(g), weighted, (((0,), (0,)), ((), ())), preferred_element_type=F32)
        sf_ref[g] = (tot_wide[0:1, lanes] + tot_wide[1:2, lanes]) * s_in + s_local
        y = (y_off[:, :gp] * wide[:, wlanes(1)] + y_off[:, gp:] * wide[:, wlanes(2)]
             + xf * (dskip_ref[:, lanes] + wide_diag[:, lanes]))

        pairs = []
        for j in range(hpg // 2):
            ms = []
            for h in (g * hpg + 2 * j, g * hpg + 2 * j + 1):
                hb = n_heads + h
                df = (jnp.broadcast_to(decc[:, h:h + 1], (CHUNK, CHUNK))
                      - jnp.broadcast_to(decr[h:h + 1, :], (CHUNK, CHUNK)))
                db = (jnp.broadcast_to(decc[:, hb:hb + 1], (CHUNK, CHUNK))
                      - jnp.broadcast_to(decr[hb:hb + 1, :], (CHUNK, CHUNK)))
                ms.append((cbs[g] * jnp.exp(jnp.where(fwd_sees, df, db))).astype(BF16))
            xp = x_ref[0, :, g * gp + 2 * j * SSM_HEADDIM:g * gp + (2 * j + 2) * SSM_HEADDIM]
            zero = jnp.zeros_like(xp)
            block_diag = jnp.concatenate([jnp.where(left_head, xp, zero), jnp.where(left_head, zero, xp)], axis=0)
            pairs.append(_dot(jnp.concatenate(ms, axis=1), block_diag))
        y_ref[0, :, lanes] = (y + jnp.concatenate(pairs, axis=1)).astype(y_ref.dtype)


def _ssd(xbc, decc, decr, cols, dskip, *, n_x_chunks, d_inner):
    bsz, frame, c = xbc.shape
    n_heads = decc.shape[2] // 4
    hpg = n_heads // SSM_GROUPS
    gp = hpg * SSM_HEADDIM
    gn = SSM_GROUPS * D_STATE
    nc = n_x_chunks + 1
    assert c == d_inner + 2 * gn and d_inner % gn == 0
    n_pairs = n_x_chunks // 2
    pair_block = lambda shape, col=0: pl.BlockSpec(
        (1,) + shape, lambda b, i: (b, n_pairs - 1 - i) + (col,) + (0,) * (len(shape) - 2))
    params = pltpu.CompilerParams(dimension_semantics=("parallel", "arbitrary"), vmem_limit_bytes=VMEM_LIMIT)
    state_scratch = pltpu.VMEM((SSM_GROUPS, D_STATE, gp), F32)
    spread1 = _spread_matrix(n_heads)
    blank = np.zeros_like(spread1)
    quantity = lambda q: np.concatenate([spread1 if r == q else blank for r in range(4)], axis=0)
    spread3 = jnp.asarray(np.concatenate([quantity(0), quantity(1), quantity(2)], axis=1), BF16)

    sb = pl.pallas_call(
        functools.partial(_ssd_bwd_state_kernel, n_heads=n_heads, hpg=hpg),
        grid=(bsz, n_pairs),
        in_specs=[pair_block((2 * CHUNK, d_inner)), pair_block((2 * CHUNK, gn), d_inner // gn),
                  pair_block((2 * CHUNK, 4 * n_heads)), pair_block((2 * CHUNK, 4 * n_heads)),
                  _resident((4 * n_heads, d_inner)), _resident((4 * n_heads, d_inner))],
        out_specs=pair_block((2, gn, gp)),
        out_shape=jax.ShapeDtypeStruct((bsz, n_x_chunks, gn, gp), BF16),
        scratch_shapes=[state_scratch],
        compiler_params=params,
        name="ssd_bwd_state",
    )(xbc, xbc, decc, cols, jnp.asarray(quantity(3), BF16), jnp.asarray(quantity(1), BF16))

    fwd = lambda i: (i + n_x_chunks) % nc
    row_block = lambda n: pl.BlockSpec((1, CHUNK, n), lambda b, i: (b, fwd(i), 0))
    state_block = pl.BlockSpec((1, 1, gn, gp), lambda b, i: (b, jnp.minimum(fwd(i), n_x_chunks - 1), 0, 0))
    return pl.pallas_call(
        functools.partial(_ssd_main_kernel, d_inner=d_inner, n_heads=n_heads, hpg=hpg),
        grid=(bsz, nc),
        in_specs=[row_block(c), row_block(4 * n_heads),
                  pl.BlockSpec((1, 2 * n_heads, CHUNK), lambda b, i: (b, 0, fwd(i))), row_block(4 * n_heads),
                  state_block, _resident(dskip.shape), _resident(spread3.shape), _resident(spread1.shape)],
        out_specs=row_block(d_inner),
        out_shape=jax.ShapeDtypeStruct((bsz, frame, d_inner), BF16),
        scratch_shapes=[state_scratch],
        compiler_params=params,
        name="ssd_main",
    )(xbc, decc, decr, cols, sb, dskip, spread3, jnp.asarray(spread1, BF16))


def _tail_kernel(x_ref, y_ref, z_ref, gates_ref, a0_prev_ref, a0_ref, a0_next_ref, a_prev_ref, a_ref, a_next_ref,
                 cw_ref, cb_ref, lng_ref, lnb_ref, wco_ref, nw_ref, wso_ref, wo_ref, nffn_ref,
                 wg_ref, wu_ref, wd_ref, nfin_ref, o_ref, buf_ref, conv_ref, h_ref, act_ref, hn_ref):
    tm, d = x_ref.shape[1], x_ref.shape[2]
    d_inner = z_ref.shape[2]
    gc = d_inner // SSM_GROUPS
    n_tiles = buf_ref.shape[0]

    def fill_window(p_ref, m_ref, n_ref):
        for t in range(n_tiles):
            sl = slice(t * LANES, (t + 1) * LANES)
            buf_ref[t, 0:HALO, :] = p_ref[0, :, sl].astype(F32)
            buf_ref[t, HALO:HALO + tm, :] = m_ref[0, :, sl].astype(F32)
            buf_ref[t, HALO + tm:, :] = n_ref[0, :, sl].astype(F32)

    n_units = 2 * n_tiles

    def conv_unit(u):
        t, row0 = u // 2, (u % 2) * (tm // 2)
        acc = _dwconv_tile(buf_ref, cw_ref, cb_ref, CONF_KERNEL, t, tm // 2, row0)
        conv_ref[row0:row0 + tm // 2, t * LANES:(t + 1) * LANES] = acc
        return acc

    def norm_act():
        v = conv_ref[...]
        mu = jnp.mean(v, axis=-1, keepdims=True)
        vc = v - mu
        var = jnp.mean(vc * vc, axis=-1, keepdims=True)
        h = _silu(vc * lax.rsqrt(var + EPS) * lng_ref[...] + lnb_ref[...])
        h_ref[...] = h.astype(BF16)
        return h

    @pl.when(pl.program_id(1) == 0)
    def _():
        fill_window(a0_prev_ref, a0_ref, a0_next_ref)
        for u in range(n_units):
            conv_unit(u)
        norm_act()

    y_conv = _dot(h_ref[...], wco_ref[...])
    fill_window(a_prev_ref, a_ref, a_next_ref)

    v = y_ref[0].astype(F32) * _silu(z_ref[0].astype(F32))
    parts = []
    for g in range(SSM_GROUPS):
        vg = v[:, g * gc:(g + 1) * gc]
        ms = jnp.mean(vg * vg, axis=-1, keepdims=True)
        parts.append((vg * lax.rsqrt(ms + EPS) * nw_ref[:, g * gc:(g + 1) * gc]).astype(BF16))
    y_ssm = _dot(jnp.concatenate(parts, axis=1), wso_ref[...])

    merged = gates_ref[0, :, :d].astype(F32) * y_conv + gates_ref[0, :, d:].astype(F32) * y_ssm
    hs = x_ref[0] + _dot(merged.astype(BF16), wo_ref[...])

    ms = jnp.mean(hs * hs, axis=-1, keepdims=True)
    hn_ref[...] = (hs * lax.rsqrt(ms + EPS) * nffn_ref[...]).astype(BF16)
    n_ffn_chunks = wg_ref.shape[1] // MXU_COLS
    for c in range(n_ffn_chunks):
        sl = slice(c * MXU_COLS, (c + 1) * MXU_COLS)
        hn = hn_ref[...]
        act_ref[:, sl] = (_silu(_dot(hn, wg_ref[:, sl])) * _dot(hn, wu_ref[:, sl])).astype(BF16)
        for u in range(c * n_units // n_ffn_chunks, (c + 1) * n_units // n_ffn_chunks):
            _order_after(hn_ref if c + 1 < n_ffn_chunks else act_ref, conv_unit(u))
    hs = hs + _dot(act_ref[...], wd_ref[...])
    norm_act()

    ms = jnp.mean(hs * hs, axis=-1, keepdims=True)
    o_ref[0] = hs * lax.rsqrt(ms + EPS) * nfin_ref[...]


def _tail(x, y, z, gates, a, weights, *, tm):
    bsz, seq, d = x.shape
    d_inner = y.shape[2]
    conf = a.shape[2]
    d_ff = weights[-4].shape[1]
    per = tm // HALO
    last = seq // tm - 1
    meta_rows_block = (seq + META_PAD) // HALO
    blk = lambda n: pl.BlockSpec((1, tm, n), lambda b, j: (b, j, 0))
    halo = lambda index: pl.BlockSpec((1, HALO, conf), lambda b, j: (b, index(j), 0))
    nxt = lambda j: jnp.minimum(j + 1, last)
    return pl.pallas_call(
        _tail_kernel,
        grid=(bsz, seq // tm),
        in_specs=[blk(d), blk(d_inner), blk(d_inner), blk(2 * d),
                  halo(lambda j: meta_rows_block), pl.BlockSpec((1, tm, conf), lambda b, j: (b, 0, 0)),
                  halo(lambda j: per),
                  halo(lambda j: (j + 1) * per - 1), pl.BlockSpec((1, tm, conf), lambda b, j: (b, nxt(j), 0)),
                  halo(lambda j: (nxt(j) + 1) * per)]
                 + [_resident(w.shape) for w in weights],
        out_specs=blk(d),
        out_shape=jax.ShapeDtypeStruct((bsz, seq, d), F32),
        scratch_shapes=[pltpu.VMEM((conf // LANES, tm + 2 * HALO, LANES), F32), pltpu.VMEM((tm, conf), F32),
                        pltpu.VMEM((tm, conf), BF16), pltpu.VMEM((tm, d_ff), BF16), pltpu.VMEM((tm, d), BF16)],
        compiler_params=pltpu.CompilerParams(dimension_semantics=("parallel", "arbitrary"),
                                             vmem_limit_bytes=VMEM_LIMIT),
        name="tail",
    )(x, y, z, gates, a, a, a, a, a, a, *weights)


def kernel(x, meta_tokens, norm_mix, w_in, conv_dw_w, conv_dw_b, conv_ln_g, conv_ln_b, conv_out_w,
           ssm_conv_w, ssm_conv_b, dt_bias_f, dt_bias_b, a_log_f, a_log_b, ssm_d, ssm_norm_w, ssm_out_w,
           w_o, norm_ffn, w_gate, w_up, w_down, norm_final):
    bsz, seq, d = x.shape
    assert norm_mix.shape[0] == 1 and seq % ROW_TILE == 0 and meta_tokens.shape[0] == N_META
    conf = conv_dw_w.shape[2]
    d_inner = ssm_norm_w.shape[1]
    n_heads = ssm_d.shape[1]
    n_xbc = ssm_conv_w.shape[2]
    assert conf == d

    w = w_in[0].astype(BF16)
    o_dt = 2 * conf + d_inner + n_xbc
    o_g = o_dt + 2 * n_heads
    dt_bias = jnp.concatenate([dt_bias_f[0], dt_bias_b[0]])
    a_log = jnp.concatenate([a_log_f[0], a_log_b[0]])
    meta_chunk = jnp.concatenate([jnp.zeros((META_PAD, d), x.dtype), meta_tokens.astype(x.dtype)], axis=0)

    a, z, xbc, decc, decr, cols, gates = _inproj(
        x, meta_chunk, norm_mix, w, w[:, o_g:], dt_bias[None, :], dt_bias[:, None], a_log[None, :], a_log[:, None],
        ssm_conv_w[0], ssm_conv_b, tm=ROW_TILE, conf=conf, d_inner=d_inner)
    y = _ssd(xbc, decc, decr, cols, jnp.repeat(ssm_d[0], SSM_HEADDIM)[None, :],
             n_x_chunks=seq // CHUNK, d_inner=d_inner)
    tail_weights = [conv_dw_w[0], conv_dw_b, conv_ln_g, conv_ln_b, conv_out_w[0].astype(BF16), ssm_norm_w,
                    ssm_out_w[0].astype(BF16), w_o[0].astype(BF16), norm_ffn, w_gate[0].astype(BF16),
                    w_up[0].astype(BF16), w_down[0].astype(BF16), norm_final[None, :]]
    return _tail(x, y, z, gates, a, tail_weights, tm=ROW_TILE)
```

```python
import functools

import jax
import jax.numpy as jnp
import numpy as np
from jax import lax
from jax.experimental import pallas as pl
from jax.experimental.pallas import tpu as pltpu

F32 = jnp.float32
BF16 = jnp.bfloat16

N_META = 16
CHUNK = 128
META_PAD = CHUNK - N_META
CONF_KERNEL = 31
SSM_CONV = 7
SSM_HEADDIM = 64
SSM_GROUPS = 4
D_STATE = 128
EPS = 1e-6
LANES = 128
MXU_COLS = 2 * LANES
HALO = 16
ROW_TILE = 2 * CHUNK
VMEM_LIMIT = 56 * 1024 * 1024


def _sigmoid(v):
    return 1.0 / (1.0 + jnp.exp(-v))


def _silu(v):
    return v * _sigmoid(v)


def _softplus(v):
    return jnp.maximum(v, 0.0) + jnp.log1p(jnp.exp(-jnp.abs(v)))


def _dot(a, b):
    return jnp.dot(a, b, preferred_element_type=F32)


def _split3(v):
    hi = v.astype(BF16)
    rest = v - hi.astype(F32)
    mid = rest.astype(BF16)
    return hi, mid, (rest - mid.astype(F32)).astype(BF16)


def _running_sum_cols(tri, v):
    hi, mid, lo = _split3(v)
    return _dot(tri, hi) + _dot(tri, mid) + _dot(tri, lo)


def _running_sum_rows(v, tri):
    n = v.shape[0]
    r = _dot(jnp.concatenate(_split3(v), axis=0), tri)
    return r[:n] + r[n:2 * n] + r[2 * n:]


def _resident(shape):
    nd = len(shape)
    return pl.BlockSpec(shape, lambda *_: (0,) * nd, pipeline_mode=pl.Buffered(1))


def _dwconv_tile(buf_ref, w_ref, b_ref, ksize, t, rows, row0=0):
    half = (ksize - 1) // 2
    sl = slice(t * LANES, (t + 1) * LANES)
    acc = jnp.broadcast_to(b_ref[:, sl], (rows, LANES))
    for k in range(ksize):
        start = row0 + HALO - half + k
        acc = acc + buf_ref[t, start:start + rows, :] * w_ref[k:k + 1, sl]
    return acc


def _order_after(dst_ref, src):
    part = None
    for r in range(0, src.shape[0], HALO):
        for c in range(0, src.shape[1], LANES):
            piece = src[r:r + HALO, c:c + LANES]
            part = piece if part is None else part + piece
    zero = pltpu.bitcast((pltpu.bitcast(part, jnp.uint32) >> 16) >> 16, F32)
    dst_ref[0:HALO, 0:LANES] = dst_ref[0:HALO, 0:LANES] + zero.astype(dst_ref.dtype)


def _inproj_kernel(prev_ref, main_ref, next_ref, meta_ref, g_ref, w_ref, w_gates_ref, dtb_ref, dtbt_ref,
                   alog_row_ref, alog_col_ref, cw_ref, cb_ref,
                   a_ref, z_ref, xbc_ref, decc_ref, decr_ref, cols_ref, gates_ref, hs_ref, hn_ref, buf_ref,
                   *, n_x_tiles, conf, d_inner):
    tm = main_ref.shape[1]
    n_xbc = xbc_ref.shape[2]
    n_dt = dtb_ref.shape[1]
    o_z = 2 * conf
    o_xbc = o_z + d_inner
    o_dt = o_xbc + n_xbc
    j = pl.program_id(1)

    @pl.when(j < n_x_tiles)
    def _():
        hs_ref[0:HALO, :] = jnp.where(j == 0, meta_ref[META_PAD:CHUNK, :], prev_ref[0])
        hs_ref[HALO:HALO + tm, :] = main_ref[0]
        hs_ref[HALO + tm:, :] = jnp.where(j == n_x_tiles - 1, 0.0, next_ref[0])

    @pl.when(j == n_x_tiles)
    def _():
        hs_ref[0:HALO, :] = jnp.zeros((HALO, hs_ref.shape[1]), F32)
        hs_ref[HALO:HALO + CHUNK, :] = meta_ref[...]
        hs_ref[HALO + CHUNK:HALO + CHUNK + HALO, :] = next_ref[0]
        hs_ref[HALO + CHUNK + HALO:, :] = jnp.zeros((tm - CHUNK, hs_ref.shape[1]), F32)

    x = hs_ref[...]
    ms = jnp.mean(x * x, axis=-1, keepdims=True)
    hn_ref[...] = (x * lax.rsqrt(ms + EPS) * g_ref[...]).astype(BF16)
    rows = slice(HALO, HALO + tm)

    def cols(k, base=0):
        return slice(base + k * MXU_COLS, base + (k + 1) * MXU_COLS)

    def xbc_chunk(k):
        r = _dot(hn_ref[...], w_ref[:, cols(k, o_xbc)])
        buf_ref[2 * k] = r[:, :LANES]
        buf_ref[2 * k + 1] = r[:, LANES:]

    def conv_tile(t):
        xbc_ref[0, :, t * LANES:(t + 1) * LANES] = _silu(
            _dwconv_tile(buf_ref, cw_ref, cb_ref, SSM_CONV, t, tm)).astype(BF16)

    def glu_chunk(k):
        a_ref[0, :, cols(k)] = (_dot(hn_ref[rows, :], w_ref[:, cols(k)])
                                * _sigmoid(_dot(hn_ref[rows, :], w_ref[:, cols(k, conf)]))).astype(BF16)

    def z_chunk(k):
        z_ref[0, :, cols(k)] = _dot(hn_ref[rows, :], w_ref[:, cols(k, o_z)]).astype(BF16)

    def gates_chunk(k):
        gates_ref[0, :, cols(k)] = _sigmoid(_dot(hn_ref[rows, :], w_gates_ref[:, cols(k)])).astype(BF16)

    others = ([functools.partial(glu_chunk, k) for k in range(conf // MXU_COLS)]
              + [functools.partial(z_chunk, k) for k in range(d_inner // MXU_COLS)]
              + [functools.partial(gates_chunk, k) for k in range(gates_ref.shape[2] // MXU_COLS)])
    def decay_terms():
        r = _dot(hn_ref[rows, :], w_ref[:, o_dt:o_dt + LANES])
        dt = _softplus(r[:, :n_dt] + dtb_ref[...])
        dtt = _softplus(r.T[:n_dt, :] + dtbt_ref[...])
        first_live = jnp.where(j == n_x_tiles, META_PAD, 0)
        dt = jnp.where(lax.broadcasted_iota(jnp.int32, (tm, 1), 0) >= first_live, dt, 0.0)
        dtt = jnp.where(lax.broadcasted_iota(jnp.int32, (1, tm), 1) >= first_live, dtt, 0.0)

        n_heads = n_dt // 2
        lower, upper, _, _ = _triangles()
        neg_a_row = -jnp.exp(alog_row_ref[...])
        neg_a_col = -jnp.exp(alog_col_ref[...])
        for c in range(tm // CHUNK):
            rs = slice(c * CHUNK, (c + 1) * CHUNK)
            dt_c, dt_r = dt[rs, :], dtt[:, rs]
            adt_c, adt_r = dt_c * neg_a_row, dt_r * neg_a_col
            cumf = _running_sum_cols(lower, adt_c[:, :n_heads])
            cumb = _running_sum_cols(upper, adt_c[:, n_heads:])
            to_edge_f = jnp.exp(cumf[CHUNK - 1:CHUNK, :] - cumf) * dt_c[:, :n_heads]
            to_edge_b = jnp.exp(cumb[0:1, :] - cumb) * dt_c[:, n_heads:]
            decc = jnp.concatenate([cumf, cumb, dt_c[:, n_heads:], jnp.zeros_like(cumf)], axis=1)
            cols = jnp.concatenate([to_edge_f, jnp.exp(cumf), jnp.exp(cumb), to_edge_b], axis=1)
            decr = jnp.concatenate(
                [_running_sum_rows(adt_r[:n_heads], upper) - jnp.log(dt_r[:n_heads]),
                 _running_sum_rows(adt_r[n_heads:], lower) - jnp.log(dt_r[n_heads:])], axis=0)
            decc_ref[0, rs, :] = decc
            cols_ref[0, rs, :] = cols.astype(BF16)
            decr_ref[0, :, rs] = decr

    n_chunks = n_xbc // MXU_COLS
    xbc_chunk(0)
    for k in range(n_chunks):
        if k + 1 < n_chunks:
            xbc_chunk(k + 1)
        share = (len(others) + n_chunks - 1 - k) // (n_chunks - k)
        for i in range(2):
            for _ in range((share + 1 - i) // 2):
                others.pop(0)()
            conv_tile(2 * k + i)
    decay_terms()


def _inproj(x, meta_chunk, g, w, w_gates, dtb, dtbt, alog_row, alog_col, cw, cb, *, tm, conf, d_inner):
    bsz, seq, d = x.shape
    frame = seq + CHUNK
    n_x_tiles = seq // tm
    per = tm // HALO
    n_xbc, n_dt, n_g = cw.shape[1], dtb.shape[1], w_gates.shape[1]
    last = n_x_tiles - 1
    main_map = lambda b, j: (b, jnp.minimum(j, last), 0)
    prev_map = lambda b, j: (b, jnp.where(jnp.logical_or(j == 0, j > last), 0, j * per - 1), 0)
    next_map = lambda b, j: (b, jnp.where(j >= last, 0, (j + 1) * per), 0)
    out_spec = lambda n: pl.BlockSpec((1, tm, n), lambda b, j: (b, j, 0))
    weights = [g, w, w_gates, dtb, dtbt, alog_row, alog_col, cw, cb]
    return pl.pallas_call(
        functools.partial(_inproj_kernel, n_x_tiles=n_x_tiles, conf=conf, d_inner=d_inner),
        grid=(bsz, n_x_tiles + 1),
        in_specs=[pl.BlockSpec((1, HALO, d), prev_map), pl.BlockSpec((1, tm, d), main_map),
                  pl.BlockSpec((1, HALO, d), next_map), _resident(meta_chunk.shape)]
                 + [_resident(v.shape) for v in weights],
        out_specs=[out_spec(conf), out_spec(d_inner), out_spec(n_xbc), out_spec(2 * n_dt),
                   pl.BlockSpec((1, n_dt, tm), lambda b, j: (b, 0, j)), out_spec(2 * n_dt), out_spec(n_g)],
        out_shape=[jax.ShapeDtypeStruct((bsz, frame, conf), BF16), jax.ShapeDtypeStruct((bsz, frame, d_inner), BF16),
                   jax.ShapeDtypeStruct((bsz, frame, n_xbc), BF16), jax.ShapeDtypeStruct((bsz, frame, 2 * n_dt), F32),
                   jax.ShapeDtypeStruct((bsz, n_dt, frame), F32), jax.ShapeDtypeStruct((bsz, frame, 2 * n_dt), BF16),
                   jax.ShapeDtypeStruct((bsz, frame, n_g), BF16)],
        scratch_shapes=[pltpu.VMEM((tm + 2 * HALO, d), F32), pltpu.VMEM((tm + 2 * HALO, d), BF16),
                        pltpu.VMEM((n_xbc // LANES, tm + 2 * HALO, LANES), F32)],
        compiler_params=pltpu.CompilerParams(dimension_semantics=("parallel", "parallel"),
                                             vmem_limit_bytes=VMEM_LIMIT),
        name="inproj",
    )(x, x, x, meta_chunk, *weights)


def _spread_matrix(n):
    return np.kron(np.eye(n, dtype=np.float32), np.ones((1, SSM_HEADDIM), np.float32))


def _triangles():
    li = lax.broadcasted_iota(jnp.int32, (CHUNK, CHUNK), 0)
    si = lax.broadcasted_iota(jnp.int32, (CHUNK, CHUNK), 1)
    return (li >= si).astype(F32).astype(BF16), (li <= si).astype(F32).astype(BF16), li, si


def _hi_lo_rows(v):
    hi = v.astype(BF16).astype(F32)
    row = lax.broadcasted_iota(jnp.int32, (16, v.shape[1]), 0)
    return jnp.where(row == 0, hi, jnp.where(row == 1, v - hi, 0.0)).astype(BF16)


def _exp_lanes(row, lo, hi):
    lane = lax.broadcasted_iota(jnp.int32, row.shape, 1)
    return jnp.exp(jnp.where(jnp.logical_and(lane >= lo, lane < hi), row, 0.0))


def _ssd_bwd_state_kernel(x_ref, b_ref, decc_ref, cols_ref, spread_edge_ref, spread_tot_ref, sb_ref, state_ref,
                          *, n_heads, hpg):
    @pl.when(pl.program_id(1) == 0)
    def _():
        state_ref[...] = jnp.zeros_like(state_ref)

    gp = hpg * SSM_HEADDIM
    chunks = []
    for c in (1, 0):
        rows = slice(c * CHUNK, (c + 1) * CHUNK)
        wide = _dot(cols_ref[0, rows, :], spread_edge_ref[...])
        total = _exp_lanes(decc_ref[0, c * CHUNK:c * CHUNK + 1, :], n_heads, 2 * n_heads)
        chunks.append((c, rows, wide, _dot(_hi_lo_rows(total), spread_tot_ref[...])))
    for g in range(SSM_GROUPS):
        lanes = slice(g * gp, (g + 1) * gp)
        state = state_ref[g]
        for c, rows, wide, tot_wide in chunks:
            weighted = (x_ref[0, rows, lanes].astype(F32) * wide[:, lanes]).astype(BF16)
            s_local = lax.dot_general(b_ref[0, rows, g * D_STATE:(g + 1) * D_STATE], weighted,
                                      (((0,), (0,)), ((), ())), preferred_element_type=F32)
            sb_ref[0, c, g * D_STATE:(g + 1) * D_STATE, :] = state.astype(sb_ref.dtype)
            state = (tot_wide[0:1, lanes] + tot_wide[1:2, lanes]) * state + s_local
        state_ref[g] = state


def _ssd_main_kernel(x_ref, decc_ref, decr_ref, cols_ref, sb_ref, dskip_ref, spread3_ref, spread1_ref,
                     y_ref, sf_ref, *, d_inner, n_heads, hpg):
    @pl.when(pl.program_id(1) == 0)
    def _():
        sf_ref[...] = jnp.zeros_like(sf_ref)

    gp = hpg * SSM_HEADDIM
    _, _, li, si = _triangles()
    fwd_sees = li >= si
    on_diag = li == si
    left_head = lax.broadcasted_iota(jnp.int32, (CHUNK, 2 * SSM_HEADDIM), 1) < SSM_HEADDIM

    def b_of(g):
        return x_ref[0, :, d_inner + g * D_STATE:d_inner + (g + 1) * D_STATE]

    def c_of(g):
        return x_ref[0, :, d_inner + (SSM_GROUPS + g) * D_STATE:d_inner + (SSM_GROUPS + g + 1) * D_STATE]

    decc = decc_ref[0]
    decr = decr_ref[0]
    cbs, diag_cols = [], []
    for g in range(SSM_GROUPS):
        cb = lax.dot_general(c_of(g), b_of(g), (((1,), (1,)), ((), ())), preferred_element_type=F32)
        cbs.append(cb)
        cb_diag = jnp.sum(jnp.where(on_diag, cb, 0.0), axis=1, keepdims=True)
        diag_cols.append(cb_diag * decc[:, 2 * n_heads + g * hpg:2 * n_heads + (g + 1) * hpg])
    wide = _dot(cols_ref[0], spread3_ref[...])
    wide_diag = _dot(jnp.concatenate(diag_cols, axis=1).astype(BF16), spread1_ref[...])
    tot_wide = _dot(_hi_lo_rows(_exp_lanes(decc[CHUNK - 1:CHUNK, :], 0, n_heads)), spread3_ref[:, :d_inner])

    for g in range(SSM_GROUPS):
        lanes = slice(g * gp, (g + 1) * gp)
        wlanes = lambda q: slice(q * d_inner + g * gp, q * d_inner + (g + 1) * gp)
        xf = x_ref[0, :, lanes].astype(F32)
        s_in = sf_ref[g]
        y_off = _dot(c_of(g), jnp.concatenate(
            [s_in.astype(BF16), sb_ref[0, 0, g * D_STATE:(g + 1) * D_STATE, :]], axis=1))
        weighted = (xf * wide[:, wlanes(0)]).astype(BF16)
        s_local = lax.dot_general(b_of(g), weighted, (((0,), (0,)), ((), ())), preferred_element_type=F32)
        sf_ref[g] = (tot_wide[0:1, lanes] + tot_wide[1:2, lanes]) * s_in + s_local
        y = (y_off[:, :gp] * wide[:, wlanes(1)] + y_off[:, gp:] * wide[:, wlanes(2)]
             + xf * (dskip_ref[:, lanes] + wide_diag[:, lanes]))

        pairs = []
        for j in range(hpg // 2):
            ms = []
            for h in (g * hpg + 2 * j, g * hpg + 2 * j + 1):
                hb = n_heads + h
                df = (jnp.broadcast_to(decc[:, h:h + 1], (CHUNK, CHUNK))
                      - jnp.broadcast_to(decr[h:h + 1, :], (CHUNK, CHUNK)))
                db = (jnp.broadcast_to(decc[:, hb:hb + 1], (CHUNK, CHUNK))
                      - jnp.broadcast_to(decr[hb:hb + 1, :], (CHUNK, CHUNK)))
                ms.append((cbs[g] * jnp.exp(jnp.where(fwd_sees, df, db))).astype(BF16))
            xp = x_ref[0, :, g * gp + 2 * j * SSM_HEADDIM:g * gp + (2 * j + 2) * SSM_HEADDIM]
            zero = jnp.zeros_like(xp)
            block_diag = jnp.concatenate([jnp.where(left_head, xp, zero), jnp.where(left_head, zero, xp)], axis=0)
            pairs.append(_dot(jnp.concatenate(ms, axis=1), block_diag))
        y_ref[0, :, lanes] = (y + jnp.concatenate(pairs, axis=1)).astype(y_ref.dtype)


def _ssd(xbc, decc, decr, cols, dskip, *, n_x_chunks, d_inner):
    bsz, frame, c = xbc.shape
    n_heads = decc.shape[2] // 4
    hpg = n_heads // SSM_GROUPS
    gp = hpg * SSM_HEADDIM
    gn = SSM_GROUPS * D_STATE
    nc = n_x_chunks + 1
    assert c == d_inner + 2 * gn and d_inner % gn == 0
    n_pairs = n_x_chunks // 2
    pair_block = lambda shape, col=0: pl.BlockSpec(
        (1,) + shape, lambda b, i: (b, n_pairs - 1 - i) + (col,) + (0,) * (len(shape) - 2))
    params = pltpu.CompilerParams(dimension_semantics=("parallel", "arbitrary"), vmem_limit_bytes=VMEM_LIMIT)
    state_scratch = pltpu.VMEM((SSM_GROUPS, D_STATE, gp), F32)
    spread1 = _spread_matrix(n_heads)
    blank = np.zeros_like(spread1)
    quantity = lambda q: np.concatenate([spread1 if r == q else blank for r in range(4)], axis=0)
    spread3 = jnp.asarray(np.concatenate([quantity(0), quantity(1), quantity(2)], axis=1), BF16)

    sb = pl.pallas_call(
        functools.partial(_ssd_bwd_state_kernel, n_heads=n_heads, hpg=hpg),
        grid=(bsz, n_pairs),
        in_specs=[pair_block((2 * CHUNK, d_inner)), pair_block((2 * CHUNK, gn), d_inner // gn),
                  pair_block((2 * CHUNK, 4 * n_heads)), pair_block((2 * CHUNK, 4 * n_heads)),
                  _resident((4 * n_heads, d_inner)), _resident((4 * n_heads, d_inner))],
        out_specs=pair_block((2, gn, gp)),
        out_shape=jax.ShapeDtypeStruct((bsz, n_x_chunks, gn, gp), BF16),
        scratch_shapes=[state_scratch],
        compiler_params=params,
        name="ssd_bwd_state",
    )(xbc, xbc, decc, cols, jnp.asarray(quantity(3), BF16), jnp.asarray(quantity(1), BF16))

    fwd = lambda i: (i + n_x_chunks) % nc
    row_block = lambda n: pl.BlockSpec((1, CHUNK, n), lambda b, i: (b, fwd(i), 0))
    state_block = pl.BlockSpec((1, 1, gn, gp), lambda b, i: (b, jnp.minimum(fwd(i), n_x_chunks - 1), 0, 0))
    return pl.pallas_call(
        functools.partial(_ssd_main_kernel, d_inner=d_inner, n_heads=n_heads, hpg=hpg),
        grid=(bsz, nc),
        in_specs=[row_block(c), row_block(4 * n_heads),
                  pl.BlockSpec((1, 2 * n_heads, CHUNK), lambda b, i: (b, 0, fwd(i))), row_block(4 * n_heads),
                  state_block, _resident(dskip.shape), _resident(spread3.shape), _resident(spread1.shape)],
        out_specs=row_block(d_inner),
        out_shape=jax.ShapeDtypeStruct((bsz, frame, d_inner), BF16),
        scratch_shapes=[state_scratch],
        compiler_params=params,
        name="ssd_main",
    )(xbc, decc, decr, cols, sb, dskip, spread3, jnp.asarray(spread1, BF16))


def _merge_kernel(x_ref, y_ref, z_ref, gates_ref, a_prev_ref, a_ref, a_next_ref,
                  cw_ref, cb_ref, lng_ref, lnb_ref, wco_ref, nw_ref, wso_ref, wo_ref, o_ref, buf_ref, conv_ref):
    tm, d = x_ref.shape[1], x_ref.shape[2]
    d_inner = z_ref.shape[2]
    gc = d_inner // SSM_GROUPS

    for t in range(buf_ref.shape[0]):
        sl = slice(t * LANES, (t + 1) * LANES)
        buf_ref[t, 0:HALO, :] = a_prev_ref[0, :, sl].astype(F32)
        buf_ref[t, HALO:HALO + tm, :] = a_ref[0, :, sl].astype(F32)
        buf_ref[t, HALO + tm:, :] = a_next_ref[0, :, sl].astype(F32)
    for t in range(buf_ref.shape[0]):
        for row0 in range(0, tm, CHUNK):
            conv_ref[row0:row0 + CHUNK, t * LANES:(t + 1) * LANES] = _dwconv_tile(
                buf_ref, cw_ref, cb_ref, CONF_KERNEL, t, CHUNK, row0)
    v = conv_ref[...]
    mu = jnp.mean(v, axis=-1, keepdims=True)
    vc = v - mu
    var = jnp.mean(vc * vc, axis=-1, keepdims=True)
    h = _silu(vc * lax.rsqrt(var + EPS) * lng_ref[...] + lnb_ref[...]).astype(BF16)
    y_conv = _dot(h, wco_ref[...])

    v = y_ref[0].astype(F32) * _silu(z_ref[0].astype(F32))
    parts = []
    for g in range(SSM_GROUPS):
        vg = v[:, g * gc:(g + 1) * gc]
        ms = jnp.mean(vg * vg, axis=-1, keepdims=True)
        parts.append((vg * lax.rsqrt(ms + EPS) * nw_ref[:, g * gc:(g + 1) * gc]).astype(BF16))
    y_ssm = _dot(jnp.concatenate(parts, axis=1), wso_ref[...])

    merged = gates_ref[0, :, :d].astype(F32) * y_conv + gates_ref[0, :, d:].astype(F32) * y_ssm
    o_ref[0] = x_ref[0] + _dot(merged.astype(BF16), wo_ref[...])


def _ffn_kernel(hs_ref, nffn_ref, wg_ref, wu_ref, wd_ref, nfin_ref, o_ref):
    hs = hs_ref[0]
    ms = jnp.mean(hs * hs, axis=-1, keepdims=True)
    hn = (hs * lax.rsqrt(ms + EPS) * nffn_ref[...]).astype(BF16)
    act = (_silu(_dot(hn, wg_ref[...])) * _dot(hn, wu_ref[...])).astype(BF16)
    hs = hs + _dot(act, wd_ref[...])
    ms = jnp.mean(hs * hs, axis=-1, keepdims=True)
    o_ref[0] = hs * lax.rsqrt(ms + EPS) * nfin_ref[...]


def _merge(x, y, z, gates, a, weights, *, tm):
    bsz, seq, d = x.shape
    d_inner = y.shape[2]
    conf = a.shape[2]
    per = tm // HALO
    meta_rows_block = (seq + META_PAD) // HALO
    blk = lambda n: pl.BlockSpec((1, tm, n), lambda b, j: (b, j, 0))
    prev_map = lambda b, j: (b, jnp.where(j == 0, meta_rows_block, j * per - 1), 0)
    next_map = lambda b, j: (b, (j + 1) * per, 0)
    return pl.pallas_call(
        _merge_kernel,
        grid=(bsz, seq // tm),
        in_specs=[blk(d), blk(d_inner), blk(d_inner), blk(2 * d),
                  pl.BlockSpec((1, HALO, conf), prev_map), blk(conf), pl.BlockSpec((1, HALO, conf), next_map)]
                 + [_resident(w.shape) for w in weights],
        out_specs=blk(d),
        out_shape=jax.ShapeDtypeStruct((bsz, seq, d), F32),
        scratch_shapes=[pltpu.VMEM((conf // LANES, tm + 2 * HALO, LANES), F32), pltpu.VMEM((tm, conf), F32)],
        compiler_params=pltpu.CompilerParams(dimension_semantics=("parallel", "parallel"),
                                             vmem_limit_bytes=VMEM_LIMIT),
        name="merge",
    )(x, y, z, gates, a, a, a, *weights)


def _ffn(hs, weights, *, tm):
    bsz, seq, d = hs.shape
    blk = pl.BlockSpec((1, tm, d), lambda b, j: (b, j, 0))
    return pl.pallas_call(
        _ffn_kernel,
        grid=(bsz, seq // tm),
        in_specs=[blk] + [_resident(w.shape) for w in weights],
        out_specs=blk,
        out_shape=jax.ShapeDtypeStruct((bsz, seq, d), F32),
        compiler_params=pltpu.CompilerParams(dimension_semantics=("parallel", "parallel"),
                                             vmem_limit_bytes=VMEM_LIMIT),
        name="ffn",
    )(hs, *weights)


def kernel(x, meta_tokens, norm_mix, w_in, conv_dw_w, conv_dw_b, conv_ln_g, conv_ln_b, conv_out_w,
           ssm_conv_w, ssm_conv_b, dt_bias_f, dt_bias_b, a_log_f, a_log_b, ssm_d, ssm_norm_w, ssm_out_w,
           w_o, norm_ffn, w_gate, w_up, w_down, norm_final):
    bsz, seq, d = x.shape
    assert norm_mix.shape[0] == 1 and seq % ROW_TILE == 0 and meta_tokens.shape[0] == N_META
    conf = conv_dw_w.shape[2]
    d_inner = ssm_norm_w.shape[1]
    n_heads = ssm_d.shape[1]
    n_xbc = ssm_conv_w.shape[2]
    assert conf == d

    w = w_in[0].astype(BF16)
    o_dt = 2 * conf + d_inner + n_xbc
    o_g = o_dt + 2 * n_heads
    dt_bias = jnp.concatenate([dt_bias_f[0], dt_bias_b[0]])
    a_log = jnp.concatenate([a_log_f[0], a_log_b[0]])
    meta_chunk = jnp.concatenate([jnp.zeros((META_PAD, d), x.dtype), meta_tokens.astype(x.dtype)], axis=0)

    a, z, xbc, decc, decr, cols, gates = _inproj(
        x, meta_chunk, norm_mix, w, w[:, o_g:], dt_bias[None, :], dt_bias[:, None], a_log[None, :], a_log[:, None],
        ssm_conv_w[0], ssm_conv_b, tm=ROW_TILE, conf=conf, d_inner=d_inner)
    y = _ssd(xbc, decc, decr, cols, jnp.repeat(ssm_d[0], SSM_HEADDIM)[None, :],
             n_x_chunks=seq // CHUNK, d_inner=d_inner)
    hs = _merge(x, y, z, gates, a,
                [conv_dw_w[0], conv_dw_b, conv_ln_g, conv_ln_b, conv_out_w[0].astype(BF16), ssm_norm_w,
                 ssm_out_w[0].astype(BF16), w_o[0].astype(BF16)], tm=ROW_TILE)
    return _ffn(hs, [norm_ffn, w_gate[0].astype(BF16), w_up[0].astype(BF16), w_down[0].astype(BF16),
                     norm_final[None, :]], tm=ROW_TILE)
```

```python
import functools

import jax
import jax.numpy as jnp
import numpy as np
from jax import lax
from jax.experimental import pallas as pl
from jax.experimental.pallas import tpu as pltpu

F32 = jnp.float32
BF16 = jnp.bfloat16

N_META = 16
CHUNK = 128
META_PAD = CHUNK - N_META
CONF_KERNEL = 31
SSM_CONV = 7
SSM_HEADDIM = 64
SSM_GROUPS = 4
D_STATE = 128
EPS = 1e-6
LANES = 128
MXU_COLS = 2 * LANES
HALO = 16
ROW_TILE = 2 * CHUNK
STEP_TEMPORARIES_BYTES = 8 * 2 ** 20


def _vmem_limit(blocks, residents, scratch):
    size = lambda shape, dtype: int(np.prod(shape)) * jnp.dtype(dtype).itemsize
    return (2 * sum(size(*b) for b in blocks) + sum(size(*r) for r in residents) + sum(size(*s) for s in scratch)
            + STEP_TEMPORARIES_BYTES)


def _sigmoid(v):
    return 1.0 / (1.0 + jnp.exp(-v))


def _silu(v):
    return v * _sigmoid(v)


def _softplus(v):
    return jnp.maximum(v, 0.0) + jnp.log1p(jnp.exp(-jnp.abs(v)))


def _dot(a, b):
    return jnp.dot(a, b, preferred_element_type=F32)


def _split3(v):
    hi = v.astype(BF16)
    rest = v - hi.astype(F32)
    mid = rest.astype(BF16)
    return hi, mid, (rest - mid.astype(F32)).astype(BF16)


def _running_sum_cols(tri, v):
    hi, mid, lo = _split3(v)
    return _dot(tri, hi) + _dot(tri, mid) + _dot(tri, lo)


def _running_sum_rows(v, tri):
    n = v.shape[0]
    r = _dot(jnp.concatenate(_split3(v), axis=0), tri)
    return r[:n] + r[n:2 * n] + r[2 * n:]


def _resident(shape):
    nd = len(shape)
    return pl.BlockSpec(shape, lambda *_: (0,) * nd, pipeline_mode=pl.Buffered(1))


def _dwconv_tile(buf_ref, w_ref, b_ref, ksize, t, rows, row0=0):
    half = (ksize - 1) // 2
    sl = slice(t * LANES, (t + 1) * LANES)
    acc = jnp.broadcast_to(b_ref[:, sl], (rows, LANES))
    for k in range(ksize):
        start = row0 + HALO - half + k
        acc = acc + buf_ref[t, start:start + rows, :] * w_ref[k:k + 1, sl]
    return acc


def _order_after(dst_ref, src):
    part = None
    for r in range(0, src.shape[0], HALO):
        for c in range(0, src.shape[1], LANES):
            piece = src[r:r + HALO, c:c + LANES]
            part = piece if part is None else part + piece
    zero = pltpu.bitcast((pltpu.bitcast(part, jnp.uint32) >> 16) >> 16, F32)
    dst_ref[0:HALO, 0:LANES] = dst_ref[0:HALO, 0:LANES] + zero.astype(dst_ref.dtype)


def _triangles():
    li = lax.broadcasted_iota(jnp.int32, (CHUNK, CHUNK), 0)
    si = lax.broadcasted_iota(jnp.int32, (CHUNK, CHUNK), 1)
    return (li >= si).astype(F32).astype(BF16), (li <= si).astype(F32).astype(BF16), li, si


def _inproj_kernel(prev_ref, main_ref, next_ref, meta_ref, g_ref, w_ref, w_gates_ref, dtb_ref, dtbt_ref,
                   alog_row_ref, alog_col_ref, cw_ref, cb_ref,
                   a_ref, z_ref, xbc_ref, decc_ref, decr_ref, cols_ref, gates_ref, hs_ref, hn_ref, buf_ref,
                   *, n_x_tiles, conf, d_inner):
    tm = main_ref.shape[1]
    n_xbc = xbc_ref.shape[2]
    n_dt = dtb_ref.shape[1]
    o_z = 2 * conf
    o_xbc = o_z + d_inner
    o_dt = o_xbc + n_xbc
    j = pl.program_id(1)

    @pl.when(j < n_x_tiles)
    def _():
        hs_ref[0:HALO, :] = jnp.where(j == 0, meta_ref[META_PAD:CHUNK, :], prev_ref[0])
        hs_ref[HALO:HALO + tm, :] = main_ref[0]
        hs_ref[HALO + tm:, :] = jnp.where(j == n_x_tiles - 1, 0.0, next_ref[0])

    @pl.when(j == n_x_tiles)
    def _():
        hs_ref[0:HALO, :] = jnp.zeros((HALO, hs_ref.shape[1]), F32)
        hs_ref[HALO:HALO + CHUNK, :] = meta_ref[...]
        hs_ref[HALO + CHUNK:HALO + CHUNK + HALO, :] = next_ref[0]
        hs_ref[HALO + CHUNK + HALO:, :] = jnp.zeros((tm - CHUNK, hs_ref.shape[1]), F32)

    x = hs_ref[...]
    ms = jnp.mean(x * x, axis=-1, keepdims=True)
    hn_ref[...] = (x * lax.rsqrt(ms + EPS) * g_ref[...]).astype(BF16)
    rows = slice(HALO, HALO + tm)

    def cols(k, base=0):
        return slice(base + k * MXU_COLS, base + (k + 1) * MXU_COLS)

    def xbc_chunk(k):
        r = _dot(hn_ref[...], w_ref[:, cols(k, o_xbc)])
        buf_ref[2 * k] = r[:, :LANES]
        buf_ref[2 * k + 1] = r[:, LANES:]

    def conv_tile(t):
        xbc_ref[0, :, t * LANES:(t + 1) * LANES] = _silu(
            _dwconv_tile(buf_ref, cw_ref, cb_ref, SSM_CONV, t, tm)).astype(BF16)

    def glu_chunk(k):
        a_ref[0, :, cols(k)] = (_dot(hn_ref[rows, :], w_ref[:, cols(k)])
                                * _sigmoid(_dot(hn_ref[rows, :], w_ref[:, cols(k, conf)]))).astype(BF16)

    def z_chunk(k):
        z_ref[0, :, cols(k)] = _dot(hn_ref[rows, :], w_ref[:, cols(k, o_z)]).astype(BF16)

    def gates_chunk(k):
        gates_ref[0, :, cols(k)] = _sigmoid(_dot(hn_ref[rows, :], w_gates_ref[:, cols(k)])).astype(BF16)

    def decay_terms():
        r = _dot(hn_ref[rows, :], w_ref[:, o_dt:o_dt + LANES])
        dt = _softplus(r[:, :n_dt] + dtb_ref[...])
        dtt = _softplus(r.T[:n_dt, :] + dtbt_ref[...])
        first_live = jnp.where(j == n_x_tiles, META_PAD, 0)
        dt = jnp.where(lax.broadcasted_iota(jnp.int32, (tm, 1), 0) >= first_live, dt, 0.0)
        dtt = jnp.where(lax.broadcasted_iota(jnp.int32, (1, tm), 1) >= first_live, dtt, 0.0)

        n_heads = n_dt // 2
        lower, upper, _, _ = _triangles()
        neg_a_row = -jnp.exp(alog_row_ref[...])
        neg_a_col = -jnp.exp(alog_col_ref[...])
        for c in range(tm // CHUNK):
            rs = slice(c * CHUNK, (c + 1) * CHUNK)
            dt_c, dt_r = dt[rs, :], dtt[:, rs]
            adt_c, adt_r = dt_c * neg_a_row, dt_r * neg_a_col
            cumf = _running_sum_cols(lower, adt_c[:, :n_heads])
            cumb = _running_sum_cols(upper, adt_c[:, n_heads:])
            to_edge_f = jnp.exp(cumf[CHUNK - 1:CHUNK, :] - cumf) * dt_c[:, :n_heads]
            to_edge_b = jnp.exp(cumb[0:1, :] - cumb) * dt_c[:, n_heads:]
            decc = jnp.concatenate([cumf, cumb, dt_c[:, n_heads:], jnp.zeros_like(cumf)], axis=1)
            cols = jnp.concatenate([to_edge_f, jnp.exp(cumf), jnp.exp(cumb), to_edge_b], axis=1)
            decr = jnp.concatenate(
                [_running_sum_rows(adt_r[:n_heads], upper) - jnp.log(dt_r[:n_heads]),
                 _running_sum_rows(adt_r[n_heads:], lower) - jnp.log(dt_r[n_heads:])], axis=0)
            decc_ref[0, rs, :] = decc
            cols_ref[0, rs, :] = cols.astype(BF16)
            decr_ref[0, :, rs] = decr

    others = ([functools.partial(glu_chunk, k) for k in range(conf // MXU_COLS)]
              + [functools.partial(z_chunk, k) for k in range(d_inner // MXU_COLS)]
              + [functools.partial(gates_chunk, k) for k in range(gates_ref.shape[2] // MXU_COLS)])
    n_chunks = n_xbc // MXU_COLS
    xbc_chunk(0)
    for k in range(n_chunks):
        if k + 1 < n_chunks:
            xbc_chunk(k + 1)
        share = (len(others) + n_chunks - 1 - k) // (n_chunks - k)
        for i in range(2):
            for _ in range((share + 1 - i) // 2):
                others.pop(0)()
            conv_tile(2 * k + i)
    decay_terms()


def _inproj(x, meta_chunk, g, w, w_gates, dtb, dtbt, alog_row, alog_col, cw, cb, *, tm, conf, d_inner):
    bsz, seq, d = x.shape
    frame = seq + CHUNK
    n_x_tiles = seq // tm
    per = tm // HALO
    n_xbc, n_dt, n_g = cw.shape[1], dtb.shape[1], w_gates.shape[1]
    last = n_x_tiles - 1
    main_map = lambda b, j: (b, jnp.minimum(j, last), 0)
    prev_map = lambda b, j: (b, jnp.where(jnp.logical_or(j == 0, j > last), 0, j * per - 1), 0)
    next_map = lambda b, j: (b, jnp.where(j >= last, 0, (j + 1) * per), 0)
    out_spec = lambda n: pl.BlockSpec((1, tm, n), lambda b, j: (b, j, 0))
    weights = [meta_chunk, g, w, w_gates, dtb, dtbt, alog_row, alog_col, cw, cb]
    ins = [((1, HALO, d), F32), ((1, tm, d), F32), ((1, HALO, d), F32)]
    outs = [((1, tm, conf), BF16), ((1, tm, d_inner), BF16), ((1, tm, n_xbc), BF16), ((1, tm, 2 * n_dt), F32),
            ((1, n_dt, tm), F32), ((1, tm, 2 * n_dt), BF16), ((1, tm, n_g), BF16)]
    scratch = [((tm + 2 * HALO, d), F32), ((tm + 2 * HALO, d), BF16), ((n_xbc // LANES, tm + 2 * HALO, LANES), F32)]
    return pl.pallas_call(
        functools.partial(_inproj_kernel, n_x_tiles=n_x_tiles, conf=conf, d_inner=d_inner),
        grid=(bsz, n_x_tiles + 1),
        in_specs=[pl.BlockSpec((1, HALO, d), prev_map), pl.BlockSpec((1, tm, d), main_map),
                  pl.BlockSpec((1, HALO, d), next_map)] + [_resident(v.shape) for v in weights],
        out_specs=[out_spec(conf), out_spec(d_inner), out_spec(n_xbc), out_spec(2 * n_dt),
                   pl.BlockSpec((1, n_dt, tm), lambda b, j: (b, 0, j)), out_spec(2 * n_dt), out_spec(n_g)],
        out_shape=[jax.ShapeDtypeStruct((bsz, frame, conf), BF16), jax.ShapeDtypeStruct((bsz, frame, d_inner), BF16),
                   jax.ShapeDtypeStruct((bsz, frame, n_xbc), BF16), jax.ShapeDtypeStruct((bsz, frame, 2 * n_dt), F32),
                   jax.ShapeDtypeStruct((bsz, n_dt, frame), F32), jax.ShapeDtypeStruct((bsz, frame, 2 * n_dt), BF16),
                   jax.ShapeDtypeStruct((bsz, frame, n_g), BF16)],
        scratch_shapes=[pltpu.VMEM(s, dt) for s, dt in scratch],
        compiler_params=pltpu.CompilerParams(
            dimension_semantics=("parallel", "parallel"),
            vmem_limit_bytes=_vmem_limit(ins + outs, [(v.shape, v.dtype) for v in weights], scratch)),
        name="inproj",
    )(x, x, x, *weights)


def _spread_matrix(n):
    return np.kron(np.eye(n, dtype=np.float32), np.ones((1, SSM_HEADDIM), np.float32))


def _hi_lo_rows(v):
    hi = v.astype(BF16).astype(F32)
    row = lax.broadcasted_iota(jnp.int32, (16, v.shape[1]), 0)
    return jnp.where(row == 0, hi, jnp.where(row == 1, v - hi, 0.0)).astype(BF16)


def _exp_lanes(row, lo, hi):
    lane = lax.broadcasted_iota(jnp.int32, row.shape, 1)
    return jnp.exp(jnp.where(jnp.logical_and(lane >= lo, lane < hi), row, 0.0))


def _ssd_bwd_state_kernel(x_ref, b_ref, decc_ref, cols_ref, spread_edge_ref, spread_tot_ref, sb_ref, state_ref,
                          *, n_heads, hpg):
    @pl.when(pl.program_id(1) == 0)
    def _():
        state_ref[...] = jnp.zeros_like(state_ref)

    gp = hpg * SSM_HEADDIM
    chunks = []
    for c in (1, 0):
        rows = slice(c * CHUNK, (c + 1) * CHUNK)
        wide = _dot(cols_ref[0, rows, :], spread_edge_ref[...])
        total = _exp_lanes(decc_ref[0, c * CHUNK:c * CHUNK + 1, :], n_heads, 2 * n_heads)
        chunks.append((c, rows, wide, _dot(_hi_lo_rows(total), spread_tot_ref[...])))
    for g in range(SSM_GROUPS):
        lanes = slice(g * gp, (g + 1) * gp)
        state = state_ref[g]
        for c, rows, wide, tot_wide in chunks:
            weighted = (x_ref[0, rows, lanes].astype(F32) * wide[:, lanes]).astype(BF16)
            s_local = lax.dot_general(b_ref[0, rows, g * D_STATE:(g + 1) * D_STATE], weighted,
                                      (((0,), (0,)), ((), ())), preferred_element_type=F32)
            sb_ref[0, c, g * D_STATE:(g + 1) * D_STATE, :] = state.astype(sb_ref.dtype)
            state = (tot_wide[0:1, lanes] + tot_wide[1:2, lanes]) * state + s_local
        state_ref[g] = state


def _ssd_main_kernel(x_ref, decc_ref, decr_ref, cols_ref, sb_ref, dskip_ref, spread3_ref, spread1_ref,
                     y_ref, sf_ref, *, d_inner, n_heads, hpg):
    @pl.when(pl.program_id(1) == 0)
    def _():
        sf_ref[...] = jnp.zeros_like(sf_ref)

    gp = hpg * SSM_HEADDIM
    _, _, li, si = _triangles()
    fwd_sees = li >= si
    on_diag = li == si
    left_head = lax.broadcasted_iota(jnp.int32, (CHUNK, 2 * SSM_HEADDIM), 1) < SSM_HEADDIM

    def b_of(g):
        return x_ref[0, :, d_inner + g * D_STATE:d_inner + (g + 1) * D_STATE]

    def c_of(g):
        return x_ref[0, :, d_inner + (SSM_GROUPS + g) * D_STATE:d_inner + (SSM_GROUPS + g + 1) * D_STATE]

    decc = decc_ref[0]
    decr = decr_ref[0]
    cbs, diag_cols = [], []
    for g in range(SSM_GROUPS):
        cb = lax.dot_general(c_of(g), b_of(g), (((1,), (1,)), ((), ())), preferred_element_type=F32)
        cbs.append(cb)
        cb_diag = jnp.sum(jnp.where(on_diag, cb, 0.0), axis=1, keepdims=True)
        diag_cols.append(cb_diag * decc[:, 2 * n_heads + g * hpg:2 * n_heads + (g + 1) * hpg])
    wide = _dot(cols_ref[0], spread3_ref[...])
    wide_diag = _dot(jnp.concatenate(diag_cols, axis=1).astype(BF16), spread1_ref[...])
    tot_wide = _dot(_hi_lo_rows(_exp_lanes(decc[CHUNK - 1:CHUNK, :], 0, n_heads)), spread3_ref[:, :d_inner])

    for g in range(SSM_GROUPS):
        lanes = slice(g * gp, (g + 1) * gp)
        wlanes = lambda q: slice(q * d_inner + g * gp, q * d_inner + (g + 1) * gp)
        xf = x_ref[0, :, lanes].astype(F32)
        s_in = sf_ref[g]
        y_off = _dot(c_of(g), jnp.concatenate(
            [s_in.astype(BF16), sb_ref[0, 0, g * D_STATE:(g + 1) * D_STATE, :]], axis=1))
        weighted = (xf * wide[:, wlanes(0)]).astype(BF16)
        s_local = lax.dot_general(b_of(g), weighted, (((0,), (0,)), ((), ())), preferred_element_type=F32)
        sf_ref[g] = (tot_wide[0:1, lanes] + tot_wide[1:2, lanes]) * s_in + s_local
        y = (y_off[:, :gp] * wide[:, wlanes(1)] + y_off[:, gp:] * wide[:, wlanes(2)]
             + xf * (dskip_ref[:, lanes] + wide_diag[:, lanes]))

        pairs = []
        for j in range(hpg // 2):
            ms = []
            for h in (g * hpg + 2 * j, g * hpg + 2 * j + 1):
                hb = n_heads + h
                df = (jnp.broadcast_to(decc[:, h:h + 1], (CHUNK, CHUNK))
                      - jnp.broadcast_to(decr[h:h + 1, :], (CHUNK, CHUNK)))
                db = (jnp.broadcast_to(decc[:, hb:hb + 1], (CHUNK, CHUNK))
                      - jnp.broadcast_to(decr[hb:hb + 1, :], (CHUNK, CHUNK)))
                ms.append((cbs[g] * jnp.exp(jnp.where(fwd_sees, df, db))).astype(BF16))
            xp = x_ref[0, :, g * gp + 2 * j * SSM_HEADDIM:g * gp + (2 * j + 2) * SSM_HEADDIM]
            zero = jnp.zeros_like(xp)
            block_diag = jnp.concatenate([jnp.where(left_head, xp, zero), jnp.where(left_head, zero, xp)], axis=0)
            pairs.append(_dot(jnp.concatenate(ms, axis=1), block_diag))
        y_ref[0, :, lanes] = (y + jnp.concatenate(pairs, axis=1)).astype(y_ref.dtype)


def _ssd(xbc, decc, decr, cols, dskip, *, n_x_chunks, d_inner):
    bsz, frame, c = xbc.shape
    n_heads = decc.shape[2] // 4
    hpg = n_heads // SSM_GROUPS
    gp = hpg * SSM_HEADDIM
    gn = SSM_GROUPS * D_STATE
    nc = n_x_chunks + 1
    assert c == d_inner + 2 * gn and d_inner % gn == 0
    n_pairs = n_x_chunks // 2
    pair_block = lambda shape, col=0: pl.BlockSpec(
        (1,) + shape, lambda b, i: (b, n_pairs - 1 - i) + (col,) + (0,) * (len(shape) - 2))
    state_scratch = ((SSM_GROUPS, D_STATE, gp), F32)
    spread1 = _spread_matrix(n_heads)
    blank = np.zeros_like(spread1)
    quantity = lambda q: np.concatenate([spread1 if r == q else blank for r in range(4)], axis=0)
    spread3 = jnp.asarray(np.concatenate([quantity(0), quantity(1), quantity(2)], axis=1), BF16)

    pair_blocks = [((1, 2 * CHUNK, d_inner), BF16), ((1, 2 * CHUNK, gn), BF16), ((1, 2 * CHUNK, 4 * n_heads), F32),
                   ((1, 2 * CHUNK, 4 * n_heads), BF16), ((1, 2, gn, gp), BF16)]
    sb = pl.pallas_call(
        functools.partial(_ssd_bwd_state_kernel, n_heads=n_heads, hpg=hpg),
        grid=(bsz, n_pairs),
        in_specs=[pair_block((2 * CHUNK, d_inner)), pair_block((2 * CHUNK, gn), d_inner // gn),
                  pair_block((2 * CHUNK, 4 * n_heads)), pair_block((2 * CHUNK, 4 * n_heads)),
                  _resident((4 * n_heads, d_inner)), _resident((4 * n_heads, d_inner))],
        out_specs=pair_block((2, gn, gp)),
        out_shape=jax.ShapeDtypeStruct((bsz, n_x_chunks, gn, gp), BF16),
        scratch_shapes=[pltpu.VMEM(*state_scratch)],
        compiler_params=pltpu.CompilerParams(
            dimension_semantics=("parallel", "arbitrary"),
            vmem_limit_bytes=_vmem_limit(pair_blocks, 2 * [((4 * n_heads, d_inner), BF16)], [state_scratch])),
        name="ssd_bwd_state",
    )(xbc, xbc, decc, cols, jnp.asarray(quantity(3), BF16), jnp.asarray(quantity(1), BF16))

    fwd = lambda i: (i + n_x_chunks) % nc
    row_block = lambda n: pl.BlockSpec((1, CHUNK, n), lambda b, i: (b, fwd(i), 0))
    state_block = pl.BlockSpec((1, 1, gn, gp), lambda b, i: (b, jnp.minimum(fwd(i), n_x_chunks - 1), 0, 0))
    chunk_blocks = [((1, CHUNK, c), BF16), ((1, CHUNK, 4 * n_heads), F32), ((1, 2 * n_heads, CHUNK), F32),
                    ((1, CHUNK, 4 * n_heads), BF16), ((1, 1, gn, gp), BF16), ((1, CHUNK, d_inner), BF16)]
    residents = [(dskip.shape, dskip.dtype), (spread3.shape, BF16), (spread1.shape, BF16)]
    return pl.pallas_call(
        functools.partial(_ssd_main_kernel, d_inner=d_inner, n_heads=n_heads, hpg=hpg),
        grid=(bsz, nc),
        in_specs=[row_block(c), row_block(4 * n_heads),
                  pl.BlockSpec((1, 2 * n_heads, CHUNK), lambda b, i: (b, 0, fwd(i))), row_block(4 * n_heads),
                  state_block, _resident(dskip.shape), _resident(spread3.shape), _resident(spread1.shape)],
        out_specs=row_block(d_inner),
        out_shape=jax.ShapeDtypeStruct((bsz, frame, d_inner), BF16),
        scratch_shapes=[pltpu.VMEM(*state_scratch)],
        compiler_params=pltpu.CompilerParams(
            dimension_semantics=("parallel", "arbitrary"),
            vmem_limit_bytes=_vmem_limit(chunk_blocks, residents, [state_scratch])),
        name="ssd_main",
    )(xbc, decc, decr, cols, sb, dskip, spread3, jnp.asarray(spread1, BF16))


def _tail_kernel(x_ref, y_ref, z_ref, gates_ref, a0_prev_ref, a0_ref, a0_next_ref, a_prev_ref, a_ref, a_next_ref,
                 cw_ref, cb_ref, lng_ref, lnb_ref, wco_ref, nw_ref, wso_ref, wo_ref, nffn_ref,
                 wg_ref, wu_ref, wd_ref, nfin_ref, o_ref, buf_ref, conv_ref, h_ref, act_ref, hn_ref):
    tm, d = x_ref.shape[1], x_ref.shape[2]
    d_inner = z_ref.shape[2]
    gc = d_inner // SSM_GROUPS
    n_tiles = buf_ref.shape[0]

    def fill_window(p_ref, m_ref, n_ref):
        for t in range(n_tiles):
            sl = slice(t * LANES, (t + 1) * LANES)
            buf_ref[t, 0:HALO, :] = p_ref[0, :, sl].astype(F32)
            buf_ref[t, HALO:HALO + tm, :] = m_ref[0, :, sl].astype(F32)
            buf_ref[t, HALO + tm:, :] = n_ref[0, :, sl].astype(F32)

    n_units = 2 * n_tiles

    def conv_unit(u):
        t, row0 = u // 2, (u % 2) * (tm // 2)
        acc = _dwconv_tile(buf_ref, cw_ref, cb_ref, CONF_KERNEL, t, tm // 2, row0)
        conv_ref[row0:row0 + tm // 2, t * LANES:(t + 1) * LANES] = acc
        return acc

    def norm_act():
        v = conv_ref[...]
        mu = jnp.mean(v, axis=-1, keepdims=True)
        vc = v - mu
        var = jnp.mean(vc * vc, axis=-1, keepdims=True)
        h_ref[...] = _silu(vc * lax.rsqrt(var + EPS) * lng_ref[...] + lnb_ref[...]).astype(BF16)

    @pl.when(pl.program_id(1) == 0)
    def _():
        fill_window(a0_prev_ref, a0_ref, a0_next_ref)
        for u in range(n_units):
            conv_unit(u)
        norm_act()

    y_conv = _dot(h_ref[...], wco_ref[...])
    fill_window(a_prev_ref, a_ref, a_next_ref)

    v = y_ref[0].astype(F32) * _silu(z_ref[0].astype(F32))
    parts = []
    for g in range(SSM_GROUPS):
        vg = v[:, g * gc:(g + 1) * gc]
        ms = jnp.mean(vg * vg, axis=-1, keepdims=True)
        parts.append((vg * lax.rsqrt(ms + EPS) * nw_ref[:, g * gc:(g + 1) * gc]).astype(BF16))
    y_ssm = _dot(jnp.concatenate(parts, axis=1), wso_ref[...])

    merged = gates_ref[0, :, :d].astype(F32) * y_conv + gates_ref[0, :, d:].astype(F32) * y_ssm
    hs = x_ref[0] + _dot(merged.astype(BF16), wo_ref[...])

    ms = jnp.mean(hs * hs, axis=-1, keepdims=True)
    hn_ref[...] = (hs * lax.rsqrt(ms + EPS) * nffn_ref[...]).astype(BF16)
    n_ffn_chunks = wg_ref.shape[1] // MXU_COLS
    for c in range(n_ffn_chunks):
        sl = slice(c * MXU_COLS, (c + 1) * MXU_COLS)
        hn = hn_ref[...]
        act_ref[:, sl] = (_silu(_dot(hn, wg_ref[:, sl])) * _dot(hn, wu_ref[:, sl])).astype(BF16)
        for u in range(c * n_units // n_ffn_chunks, (c + 1) * n_units // n_ffn_chunks):
            _order_after(hn_ref if c + 1 < n_ffn_chunks else act_ref, conv_unit(u))
    hs = hs + _dot(act_ref[...], wd_ref[...])
    norm_act()

    ms = jnp.mean(hs * hs, axis=-1, keepdims=True)
    o_ref[0] = hs * lax.rsqrt(ms + EPS) * nfin_ref[...]


def _tail(x, y, z, gates, a, weights, *, tm):
    bsz, seq, d = x.shape
    d_inner = y.shape[2]
    conf = a.shape[2]
    d_ff = weights[-4].shape[1]
    per = tm // HALO
    last = seq // tm - 1
    meta_rows_block = (seq + META_PAD) // HALO
    blk = lambda n: pl.BlockSpec((1, tm, n), lambda b, j: (b, j, 0))
    halo = lambda index: pl.BlockSpec((1, HALO, conf), lambda b, j: (b, index(j), 0))
    nxt = lambda j: jnp.minimum(j + 1, last)
    window = [((1, HALO, conf), BF16), ((1, tm, conf), BF16), ((1, HALO, conf), BF16)]
    blocks = [((1, tm, d), F32), ((1, tm, d_inner), BF16), ((1, tm, d_inner), BF16), ((1, tm, 2 * d), BF16),
              ((1, tm, d), F32)] + 2 * window
    scratch = [((conf // LANES, tm + 2 * HALO, LANES), F32), ((tm, conf), F32), ((tm, conf), BF16),
               ((tm, d_ff), BF16), ((tm, d), BF16)]
    return pl.pallas_call(
        _tail_kernel,
        grid=(bsz, seq // tm),
        in_specs=[blk(d), blk(d_inner), blk(d_inner), blk(2 * d),
                  halo(lambda j: meta_rows_block), pl.BlockSpec((1, tm, conf), lambda b, j: (b, 0, 0)),
                  halo(lambda j: per),
                  halo(lambda j: (j + 1) * per - 1), pl.BlockSpec((1, tm, conf), lambda b, j: (b, nxt(j), 0)),
                  halo(lambda j: (nxt(j) + 1) * per)]
                 + [_resident(w.shape) for w in weights],
        out_specs=blk(d),
        out_shape=jax.ShapeDtypeStruct((bsz, seq, d), F32),
        scratch_shapes=[pltpu.VMEM(s, dt) for s, dt in scratch],
        compiler_params=pltpu.CompilerParams(
            dimension_semantics=("parallel", "arbitrary"),
            vmem_limit_bytes=_vmem_limit(blocks, [(w.shape, w.dtype) for w in weights], scratch)),
        name="tail",
    )(x, y, z, gates, a, a, a, a, a, a, *weights)


def kernel(x, meta_tokens, norm_mix, w_in, conv_dw_w, conv_dw_b, conv_ln_g, conv_ln_b, conv_out_w,
           ssm_conv_w, ssm_conv_b, dt_bias_f, dt_bias_b, a_log_f, a_log_b, ssm_d, ssm_norm_w, ssm_out_w,
           w_o, norm_ffn, w_gate, w_up, w_down, norm_final):
    bsz, seq, d = x.shape
    assert norm_mix.shape[0] == 1 and seq % ROW_TILE == 0 and meta_tokens.shape[0] == N_META
    conf = conv_dw_w.shape[2]
    d_inner = ssm_norm_w.shape[1]
    n_heads = ssm_d.shape[1]
    n_xbc = ssm_conv_w.shape[2]
    assert conf == d

    w = w_in[0].astype(BF16)
    o_dt = 2 * conf + d_inner + n_xbc
    o_g = o_dt + 2 * n_heads
    dt_bias = jnp.concatenate([dt_bias_f[0], dt_bias_b[0]])
    a_log = jnp.concatenate([a_log_f[0], a_log_b[0]])
    meta_chunk = jnp.concatenate([jnp.zeros((META_PAD, d), x.dtype), meta_tokens.astype(x.dtype)], axis=0)

    a, z, xbc, decc, decr, cols, gates = _inproj(
        x, meta_chunk, norm_mix, w, w[:, o_g:], dt_bias[None, :], dt_bias[:, None], a_log[None, :], a_log[:, None],
        ssm_conv_w[0], ssm_conv_b, tm=ROW_TILE, conf=conf, d_inner=d_inner)
    y = _ssd(xbc, decc, decr, cols, jnp.repeat(ssm_d[0], SSM_HEADDIM)[None, :],
             n_x_chunks=seq // CHUNK, d_inner=d_inner)
    tail_weights = [conv_dw_w[0], conv_dw_b, conv_ln_g, conv_ln_b, conv_out_w[0].astype(BF16), ssm_norm_w,
                    ssm_out_w[0].astype(BF16), w_o[0].astype(BF16), norm_ffn, w_gate[0].astype(BF16),
                    w_up[0].astype(BF16), w_down[0].astype(BF16), norm_final[None, :]]
    return _tail(x, y, z, gates, a, tail_weights, tm=ROW_TILE)
```

```python
import functools

import jax
import jax.numpy as jnp
import numpy as np
from jax import lax
from jax.experimental import pallas as pl
from jax.experimental.pallas import tpu as pltpu

F32 = jnp.float32
BF16 = jnp.bfloat16

N_META = 16
CHUNK = 128
META_PAD = CHUNK - N_META
CONF_KERNEL = 31
SSM_CONV = 7
SSM_HEADDIM = 64
SSM_GROUPS = 4
D_STATE = 128
EPS = 1e-6
LANES = 128
MXU_COLS = 2 * LANES
HALO = 16
ROW_TILE = 2 * CHUNK
VMEM_LIMIT = 60 * 1024 * 1024


def _sigmoid(v):
    return 1.0 / (1.0 + jnp.exp(-v))


def _silu(v):
    return v * _sigmoid(v)


def _softplus(v):
    return jnp.maximum(v, 0.0) + jnp.log1p(jnp.exp(-jnp.abs(v)))


def _dot(a, b):
    return jnp.dot(a, b, preferred_element_type=F32)


def _split3(v):
    hi = v.astype(BF16)
    rest = v - hi.astype(F32)
    mid = rest.astype(BF16)
    return hi, mid, (rest - mid.astype(F32)).astype(BF16)


def _running_sum_cols(tri, v):
    hi, mid, lo = _split3(v)
    return _dot(tri, hi) + _dot(tri, mid) + _dot(tri, lo)


def _running_sum_rows(v, tri):
    n = v.shape[0]
    r = _dot(jnp.concatenate(_split3(v), axis=0), tri)
    return r[:n] + r[n:2 * n] + r[2 * n:]


def _resident(shape):
    nd = len(shape)
    return pl.BlockSpec(shape, lambda *_: (0,) * nd, pipeline_mode=pl.Buffered(1))


def _dwconv_tile(buf_ref, w_ref, b_ref, ksize, t, rows, row0=0):
    half = (ksize - 1) // 2
    sl = slice(t * LANES, (t + 1) * LANES)
    acc = jnp.broadcast_to(b_ref[:, sl], (rows, LANES))
    for k in range(ksize):
        start = row0 + HALO - half + k
        acc = acc + buf_ref[t, start:start + rows, :] * w_ref[k:k + 1, sl]
    return acc


def _order_after(dst_ref, src):
    part = None
    for r in range(0, src.shape[0], HALO):
        for c in range(0, src.shape[1], LANES):
            piece = src[r:r + HALO, c:c + LANES]
            part = piece if part is None else part + piece
    zero = pltpu.bitcast((pltpu.bitcast(part, jnp.uint32) >> 16) >> 16, F32)
    dst_ref[0:HALO, 0:LANES] = dst_ref[0:HALO, 0:LANES] + zero.astype(dst_ref.dtype)


def _inproj_kernel(prev_ref, main_ref, next_ref, meta_ref, g_ref, w_ref, w_gates_ref, dtb_ref, dtbt_ref,
                   alog_row_ref, alog_col_ref, cw_ref, cb_ref,
                   a_ref, z_ref, xbc_ref, decc_ref, decr_ref, cols_ref, gates_ref, hs_ref, hn_ref, buf_ref,
                   *, n_x_tiles, conf, d_inner):
    tm = main_ref.shape[1]
    n_xbc = xbc_ref.shape[2]
    n_dt = dtb_ref.shape[1]
    o_z = 2 * conf
    o_xbc = o_z + d_inner
    o_dt = o_xbc + n_xbc
    j = pl.program_id(1)

    @pl.when(j < n_x_tiles)
    def _():
        hs_ref[0:HALO, :] = jnp.where(j == 0, meta_ref[META_PAD:CHUNK, :], prev_ref[0])
        hs_ref[HALO:HALO + tm, :] = main_ref[0]
        hs_ref[HALO + tm:, :] = jnp.where(j == n_x_tiles - 1, 0.0, next_ref[0])

    @pl.when(j == n_x_tiles)
    def _():
        hs_ref[0:HALO, :] = jnp.zeros((HALO, hs_ref.shape[1]), F32)
        hs_ref[HALO:HALO + CHUNK, :] = meta_ref[...]
        hs_ref[HALO + CHUNK:HALO + CHUNK + HALO, :] = next_ref[0]
        hs_ref[HALO + CHUNK + HALO:, :] = jnp.zeros((tm - CHUNK, hs_ref.shape[1]), F32)

    x = hs_ref[...]
    ms = jnp.mean(x * x, axis=-1, keepdims=True)
    hn_ref[...] = (x * lax.rsqrt(ms + EPS) * g_ref[...]).astype(BF16)
    rows = slice(HALO, HALO + tm)

    def cols(k, base=0):
        return slice(base + k * MXU_COLS, base + (k + 1) * MXU_COLS)

    def xbc_chunk(k):
        r = _dot(hn_ref[...], w_ref[:, cols(k, o_xbc)])
        buf_ref[2 * k] = r[:, :LANES]
        buf_ref[2 * k + 1] = r[:, LANES:]

    def conv_tile(t):
        xbc_ref[0, :, t * LANES:(t + 1) * LANES] = _silu(
            _dwconv_tile(buf_ref, cw_ref, cb_ref, SSM_CONV, t, tm)).astype(BF16)

    def glu_chunk(k):
        a_ref[0, :, cols(k)] = (_dot(hn_ref[rows, :], w_ref[:, cols(k)])
                                * _sigmoid(_dot(hn_ref[rows, :], w_ref[:, cols(k, conf)]))).astype(BF16)

    def z_chunk(k):
        z_ref[0, :, cols(k)] = _dot(hn_ref[rows, :], w_ref[:, cols(k, o_z)]).astype(BF16)

    def gates_chunk(k):
        gates_ref[0, :, cols(k)] = _sigmoid(_dot(hn_ref[rows, :], w_gates_ref[:, cols(k)])).astype(BF16)

    others = ([functools.partial(glu_chunk, k) for k in range(conf // MXU_COLS)]
              + [functools.partial(z_chunk, k) for k in range(d_inner // MXU_COLS)]
              + [functools.partial(gates_chunk, k) for k in range(gates_ref.shape[2] // MXU_COLS)])
    def decay_terms():
        r = _dot(hn_ref[rows, :], w_ref[:, o_dt:o_dt + LANES])
        dt = _softplus(r[:, :n_dt] + dtb_ref[...])
        dtt = _softplus(r.T[:n_dt, :] + dtbt_ref[...])
        first_live = jnp.where(j == n_x_tiles, META_PAD, 0)
        dt = jnp.where(lax.broadcasted_iota(jnp.int32, (tm, 1), 0) >= first_live, dt, 0.0)
        dtt = jnp.where(lax.broadcasted_iota(jnp.int32, (1, tm), 1) >= first_live, dtt, 0.0)

        n_heads = n_dt // 2
        lower, upper, _, _ = _triangles()
        neg_a_row = -jnp.exp(alog_row_ref[...])
        neg_a_col = -jnp.exp(alog_col_ref[...])
        for c in range(tm // CHUNK):
            rs = slice(c * CHUNK, (c + 1) * CHUNK)
            dt_c, dt_r = dt[rs, :], dtt[:, rs]
            adt_c, adt_r = dt_c * neg_a_row, dt_r * neg_a_col
            cumf = _running_sum_cols(lower, adt_c[:, :n_heads])
            cumb = _running_sum_cols(upper, adt_c[:, n_heads:])
            to_edge_f = jnp.exp(cumf[CHUNK - 1:CHUNK, :] - cumf) * dt_c[:, :n_heads]
            to_edge_b = jnp.exp(cumb[0:1, :] - cumb) * dt_c[:, n_heads:]
            decc = jnp.concatenate([cumf, cumb, dt_c[:, n_heads:], jnp.zeros_like(cumf)], axis=1)
            cols = jnp.concatenate([to_edge_f, jnp.exp(cumf), jnp.exp(cumb), to_edge_b], axis=1)
            decr = jnp.concatenate(
                [_running_sum_rows(adt_r[:n_heads], upper) - jnp.log(dt_r[:n_heads]),
                 _running_sum_rows(adt_r[n_heads:], lower) - jnp.log(dt_r[n_heads:])], axis=0)
            decc_ref[0, rs, :] = decc
            cols_ref[0, rs, :] = cols.astype(BF16)
            decr_ref[0, :, rs] = decr

    n_chunks = n_xbc // MXU_COLS
    xbc_chunk(0)
    for k in range(n_chunks):
        if k + 1 < n_chunks:
            xbc_chunk(k + 1)
        share = (len(others) + n_chunks - 1 - k) // (n_chunks - k)
        for i in range(2):
            for _ in range((share + 1 - i) // 2):
                others.pop(0)()
            conv_tile(2 * k + i)
    decay_terms()


def _inproj(x, meta_chunk, g, w, w_gates, dtb, dtbt, alog_row, alog_col, cw, cb, *, tm, conf, d_inner):
    bsz, seq, d = x.shape
    frame = seq + CHUNK
    n_x_tiles = seq // tm
    per = tm // HALO
    n_xbc, n_dt, n_g = cw.shape[1], dtb.shape[1], w_gates.shape[1]
    last = n_x_tiles - 1
    main_map = lambda b, j: (b, jnp.minimum(j, last), 0)
    prev_map = lambda b, j: (b, jnp.where(jnp.logical_or(j == 0, j > last), 0, j * per - 1), 0)
    next_map = lambda b, j: (b, jnp.where(j >= last, 0, (j + 1) * per), 0)
    out_spec = lambda n: pl.BlockSpec((1, tm, n), lambda b, j: (b, j, 0))
    weights = [g, w, w_gates, dtb, dtbt, alog_row, alog_col, cw, cb]
    return pl.pallas_call(
        functools.partial(_inproj_kernel, n_x_tiles=n_x_tiles, conf=conf, d_inner=d_inner),
        grid=(bsz, n_x_tiles + 1),
        in_specs=[pl.BlockSpec((1, HALO, d), prev_map), pl.BlockSpec((1, tm, d), main_map),
                  pl.BlockSpec((1, HALO, d), next_map), _resident(meta_chunk.shape)]
                 + [_resident(v.shape) for v in weights],
        out_specs=[out_spec(conf), out_spec(d_inner), out_spec(n_xbc), out_spec(2 * n_dt),
                   pl.BlockSpec((1, n_dt, tm), lambda b, j: (b, 0, j)), out_spec(2 * n_dt), out_spec(n_g)],
        out_shape=[jax.ShapeDtypeStruct((bsz, frame, conf), BF16), jax.ShapeDtypeStruct((bsz, frame, d_inner), BF16),
                   jax.ShapeDtypeStruct((bsz, frame, n_xbc), BF16), jax.ShapeDtypeStruct((bsz, frame, 2 * n_dt), F32),
                   jax.ShapeDtypeStruct((bsz, n_dt, frame), F32), jax.ShapeDtypeStruct((bsz, frame, 2 * n_dt), BF16),
                   jax.ShapeDtypeStruct((bsz, frame, n_g), BF16)],
        scratch_shapes=[pltpu.VMEM((tm + 2 * HALO, d), F32), pltpu.VMEM((tm + 2 * HALO, d), BF16),
                        pltpu.VMEM((n_xbc // LANES, tm + 2 * HALO, LANES), F32)],
        compiler_params=pltpu.CompilerParams(dimension_semantics=("parallel", "parallel"),
                                             vmem_limit_bytes=VMEM_LIMIT),
        name="inproj",
    )(x, x, x, meta_chunk, *weights)


def _spread_matrix(n):
    return np.kron(np.eye(n, dtype=np.float32), np.ones((1, SSM_HEADDIM), np.float32))


def _triangles():
    li = lax.broadcasted_iota(jnp.int32, (CHUNK, CHUNK), 0)
    si = lax.broadcasted_iota(jnp.int32, (CHUNK, CHUNK), 1)
    return (li >= si).astype(F32).astype(BF16), (li <= si).astype(F32).astype(BF16), li, si


def _hi_lo_rows(v):
    hi = v.astype(BF16).astype(F32)
    row = lax.broadcasted_iota(jnp.int32, (16, v.shape[1]), 0)
    return jnp.where(row == 0, hi, jnp.where(row == 1, v - hi, 0.0)).astype(BF16)


def _exp_lanes(row, lo, hi):
    lane = lax.broadcasted_iota(jnp.int32, row.shape, 1)
    return jnp.exp(jnp.where(jnp.logical_and(lane >= lo, lane < hi), row, 0.0))


def _ssd_bwd_state_kernel(x_ref, b_ref, decc_ref, cols_ref, spread_edge_ref, spread_tot_ref, sb_ref, state_ref,
                          *, n_heads, hpg):
    @pl.when(pl.program_id(1) == 0)
    def _():
        state_ref[...] = jnp.zeros_like(state_ref)

    gp = hpg * SSM_HEADDIM
    chunks = []
    for c in (1, 0):
        rows = slice(c * CHUNK, (c + 1) * CHUNK)
        wide = _dot(cols_ref[0, rows, :], spread_edge_ref[...])
        total = _exp_lanes(decc_ref[0, c * CHUNK:c * CHUNK + 1, :], n_heads, 2 * n_heads)
        chunks.append((c, rows, wide, _dot(_hi_lo_rows(total), spread_tot_ref[...])))
    for g in range(SSM_GROUPS):
        lanes = slice(g * gp, (g + 1) * gp)
        state = state_ref[g]
        for c, rows, wide, tot_wide in chunks:
            weighted = (x_ref[0, rows, lanes].astype(F32) * wide[:, lanes]).astype(BF16)
            s_local = lax.dot_general(b_ref[0, rows, g * D_STATE:(g + 1) * D_STATE], weighted,
                                      (((0,), (0,)), ((), ())), preferred_element_type=F32)
            sb_ref[0, c, g * D_STATE:(g + 1) * D_STATE, :] = state.astype(sb_ref.dtype)
            state = (tot_wide[0:1, lanes] + tot_wide[1:2, lanes]) * state + s_local
        state_ref[g] = state


def _ssd_main_kernel(x_ref, decc_ref, decr_ref, cols_ref, sb_ref, dskip_ref, spread3_ref, spread1_ref,
                     y_ref, sf_ref, *, d_inner, n_heads, hpg):
    @pl.when(pl.program_id(1) == 0)
    def _():
        sf_ref[...] = jnp.zeros_like(sf_ref)

    gp = hpg * SSM_HEADDIM
    _, _, li, si = _triangles()
    fwd_sees = li >= si
    on_diag = li == si
    left_head = lax.broadcasted_iota(jnp.int32, (CHUNK, 2 * SSM_HEADDIM), 1) < SSM_HEADDIM

    def b_of(g):
        return x_ref[0, :, d_inner + g * D_STATE:d_inner + (g + 1) * D_STATE]

    def c_of(g):
        return x_ref[0, :, d_inner + (SSM_GROUPS + g) * D_STATE:d_inner + (SSM_GROUPS + g + 1) * D_STATE]

    decc = decc_ref[0]
    decr = decr_ref[0]
    cbs, diag_cols = [], []
    for g in range(SSM_GROUPS):
        cb = lax.dot_general(c_of(g), b_of(g), (((1,), (1,)), ((), ())), preferred_element_type=F32)
        cbs.append(cb)
        cb_diag = jnp.sum(jnp.where(on_diag, cb, 0.0), axis=1, keepdims=True)
        diag_cols.append(cb_diag * decc[:, 2 * n_heads + g * hpg:2 * n_heads + (g + 1) * hpg])
    wide = _dot(cols_ref[0], spread3_ref[...])
    wide_diag = _dot(jnp.concatenate(diag_cols, axis=1).astype(BF16), spread1_ref[...])
    tot_wide = _dot(_hi_lo_rows(_exp_lanes(decc[CHUNK - 1:CHUNK, :], 0, n_heads)), spread3_ref[:, :d_inner])

    for g in range(SSM_GROUPS):
        lanes = slice(g * gp, (g + 1) * gp)
        wlanes = lambda q: slice(q * d_inner + g * gp, q * d_inner + (g + 1) * gp)
        xf = x_ref[0, :, lanes].astype(F32)
        s_in = sf_ref[g]
        y_off = _dot(c_of(g), jnp.concatenate(
            [s_in.astype(BF16), sb_ref[0, 0, g * D_STATE:(g + 1) * D_STATE, :]], axis=1))
        weighted = (xf * wide[:, wlanes(0)]).astype(BF16)
        s_local = lax.dot_general(b_of(g), weighted, (((0,), (0,)), ((), ())), preferred_element_type=F32)
        sf_ref[g] = (tot_wide[0:1, lanes] + tot_wide[1:2, lanes]) * s_in + s_local
        y = (y_off[:, :gp] * wide[:, wlanes(1)] + y_off[:, gp:] * wide[:, wlanes(2)]
             + xf * (dskip_ref[:, lanes] + wide_diag[:, lanes]))

        pairs = []
        for j in range(hpg // 2):
            ms = []
            for h in (g * hpg + 2 * j, g * hpg + 2 * j + 1):
                hb = n_heads + h
                df = (jnp.broadcast_to(decc[:, h:h + 1], (CHUNK, CHUNK))
                      - jnp.broadcast_to(decr[h:h + 1, :], (CHUNK, CHUNK)))
                db = (jnp.broadcast_to(decc[:, hb:hb + 1], (CHUNK, CHUNK))
                      - jnp.broadcast_to(decr[hb:hb + 1, :], (CHUNK, CHUNK)))
                ms.append((cbs[g] * jnp.exp(jnp.where(fwd_sees, df, db))).astype(BF16))
            xp = x_ref[0, :, g * gp + 2 * j * SSM_HEADDIM:g * gp + (2 * j + 2) * SSM_HEADDIM]
            zero = jnp.zeros_like(xp)
            block_diag = jnp.concatenate([jnp.where(left_head, xp, zero), jnp.where(left_head, zero, xp)], axis=0)
            pairs.append(_dot(jnp.concatenate(ms, axis=1), block_diag))
        y_ref[0, :, lanes] = (y + jnp.concatenate(pairs, axis=1)).astype(y_ref.dtype)


def _ssd(xbc, decc, decr, cols, dskip, *, n_x_chunks, d_inner):
    bsz, frame, c = xbc.shape
    n_heads = decc.shape[2] // 4
    hpg = n_heads // SSM_GROUPS
    gp = hpg * SSM_HEADDIM
    gn = SSM_GROUPS * D_STATE
    nc = n_x_chunks + 1
    assert c == d_inner + 2 * gn and d_inner % gn == 0
    n_pairs = n_x_chunks // 2
    pair_block = lambda shape, col=0: pl.BlockSpec(
        (1,) + shape, lambda b, i: (b, n_pairs - 1 - i) + (col,) + (0,) * (len(shape) - 2))
    params = pltpu.CompilerParams(dimension_semantics=("parallel", "arbitrary"), vmem_limit_bytes=VMEM_LIMIT)
    state_scratch = pltpu.VMEM((SSM_GROUPS, D_STATE, gp), F32)
    spread1 = _spread_matrix(n_heads)
    blank = np.zeros_like(spread1)
    quantity = lambda q: np.concatenate([spread1 if r == q else blank for r in range(4)], axis=0)
    spread3 = jnp.asarray(np.concatenate([quantity(0), quantity(1), quantity(2)], axis=1), BF16)

    sb = pl.pallas_call(
        functools.partial(_ssd_bwd_state_kernel, n_heads=n_heads, hpg=hpg),
        grid=(bsz, n_pairs),
        in_specs=[pair_block((2 * CHUNK, d_inner)), pair_block((2 * CHUNK, gn), d_inner // gn),
                  pair_block((2 * CHUNK, 4 * n_heads)), pair_block((2 * CHUNK, 4 * n_heads)),
                  _resident((4 * n_heads, d_inner)), _resident((4 * n_heads, d_inner))],
        out_specs=pair_block((2, gn, gp)),
        out_shape=jax.ShapeDtypeStruct((bsz, n_x_chunks, gn, gp), BF16),
        scratch_shapes=[state_scratch],
        compiler_params=params,
        name="ssd_bwd_state",
    )(xbc, xbc, decc, cols, jnp.asarray(quantity(3), BF16), jnp.asarray(quantity(1), BF16))

    fwd = lambda i: (i + n_x_chunks) % nc
    row_block = lambda n: pl.BlockSpec((1, CHUNK, n), lambda b, i: (b, fwd(i), 0))
    state_block = pl.BlockSpec((1, 1, gn, gp), lambda b, i: (b, jnp.minimum(fwd(i), n_x_chunks - 1), 0, 0))
    return pl.pallas_call(
        functools.partial(_ssd_main_kernel, d_inner=d_inner, n_heads=n_heads, hpg=hpg),
        grid=(bsz, nc),
        in_specs=[row_block(c), row_block(4 * n_heads),
                  pl.BlockSpec((1, 2 * n_heads, CHUNK), lambda b, i: (b, 0, fwd(i))), row_block(4 * n_heads),
                  state_block, _resident(dskip.shape), _resident(spread3.shape), _resident(spread1.shape)],
        out_specs=row_block(d_inner),
        out_shape=jax.ShapeDtypeStruct((bsz, frame, d_inner), BF16),
        scratch_shapes=[state_scratch],
        compiler_params=params,
        name="ssd_main",
    )(xbc, decc, decr, cols, sb, dskip, spread3, jnp.asarray(spread1, BF16))


def _tail_kernel(x_ref, y_ref, z_ref, gates_ref, a0_prev_ref, a0_ref, a0_next_ref, a_prev_ref, a_ref, a_next_ref,
                 cw_ref, cb_ref, lng_ref, lnb_ref, wco_ref, nw_ref, wso_ref, wo_ref, nffn_ref,
                 wg_ref, wu_ref, wd_ref, nfin_ref, o_ref, buf_ref, conv_ref, h_ref, act_ref, hn_ref):
    tm, d = x_ref.shape[1], x_ref.shape[2]
    d_inner = z_ref.shape[2]
    gc = d_inner // SSM_GROUPS
    n_tiles = buf_ref.shape[0]

    def fill_window(p_ref, m_ref, n_ref):
        for t in range(n_tiles):
            sl = slice(t * LANES, (t + 1) * LANES)
            buf_ref[t, 0:HALO, :] = p_ref[0, :, sl].astype(F32)
            buf_ref[t, HALO:HALO + tm, :] = m_ref[0, :, sl].astype(F32)
            buf_ref[t, HALO + tm:, :] = n_ref[0, :, sl].astype(F32)

    n_units = 2 * n_tiles

    def conv_unit(u):
        t, row0 = u // 2, (u % 2) * (tm // 2)
        acc = _dwconv_tile(buf_ref, cw_ref, cb_ref, CONF_KERNEL, t, tm // 2, row0)
        conv_ref[row0:row0 + tm // 2, t * LANES:(t + 1) * LANES] = acc
        return acc

    def norm_act():
        v = conv_ref[...]
        mu = jnp.mean(v, axis=-1, keepdims=True)
        vc = v - mu
        var = jnp.mean(vc * vc, axis=-1, keepdims=True)
        h = _silu(vc * lax.rsqrt(var + EPS) * lng_ref[...] + lnb_ref[...])
        h_ref[...] = h.astype(BF16)
        return h

    @pl.when(pl.program_id(1) == 0)
    def _():
        fill_window(a0_prev_ref, a0_ref, a0_next_ref)
        for u in range(n_units):
            conv_unit(u)
        norm_act()

    y_conv = _dot(h_ref[...], wco_ref[...])
    fill_window(a_prev_ref, a_ref, a_next_ref)

    v = y_ref[0].astype(F32) * _silu(z_ref[0].astype(F32))
    parts = []
    for g in range(SSM_GROUPS):
        vg = v[:, g * gc:(g + 1) * gc]
        ms = jnp.mean(vg * vg, axis=-1, keepdims=True)
        parts.append((vg * lax.rsqrt(ms + EPS) * nw_ref[:, g * gc:(g + 1) * gc]).astype(BF16))
    y_ssm = _dot(jnp.concatenate(parts, axis=1), wso_ref[...])

    merged = gates_ref[0, :, :d].astype(F32) * y_conv + gates_ref[0, :, d:].astype(F32) * y_ssm
    hs = x_ref[0] + _dot(merged.astype(BF16), wo_ref[...])

    ms = jnp.mean(hs * hs, axis=-1, keepdims=True)
    hn_ref[...] = (hs * lax.rsqrt(ms + EPS) * nffn_ref[...]).astype(BF16)
    n_ffn_chunks = wg_ref.shape[1] // MXU_COLS
    for c in range(n_ffn_chunks):
        sl = slice(c * MXU_COLS, (c + 1) * MXU_COLS)
        hn = hn_ref[...]
        act_ref[:, sl] = (_silu(_dot(hn, wg_ref[:, sl])) * _dot(hn, wu_ref[:, sl])).astype(BF16)
        for u in range(c * n_units // n_ffn_chunks, (c + 1) * n_units // n_ffn_chunks):
            _order_after(hn_ref if c + 1 < n_ffn_chunks else act_ref, conv_unit(u))
    hs = hs + _dot(act_ref[...], wd_ref[...])
    norm_act()

    ms = jnp.mean(hs * hs, axis=-1, keepdims=True)
    o_ref[0] = hs * lax.rsqrt(ms + EPS) * nfin_ref[...]


def _tail(x, y, z, gates, a, weights, *, tm):
    bsz, seq, d = x.shape
    d_inner = y.shape[2]
    conf = a.shape[2]
    d_ff = weights[-4].shape[1]
    per = tm // HALO
    last = seq // tm - 1
    meta_rows_block = (seq + META_PAD) // HALO
    blk = lambda n: pl.BlockSpec((1, tm, n), lambda b, j: (b, j, 0))
    halo = lambda index: pl.BlockSpec((1, HALO, conf), lambda b, j: (b, index(j), 0))
    nxt = lambda j: jnp.minimum(j + 1, last)
    return pl.pallas_call(
        _tail_kernel,
        grid=(bsz, seq // tm),
        in_specs=[blk(d), blk(d_inner), blk(d_inner), blk(2 * d),
                  halo(lambda j: meta_rows_block), pl.BlockSpec((1, tm, conf), lambda b, j: (b, 0, 0)),
                  halo(lambda j: per),
                  halo(lambda j: (j + 1) * per - 1), pl.BlockSpec((1, tm, conf), lambda b, j: (b, nxt(j), 0)),
                  halo(lambda j: (nxt(j) + 1) * per)]
                 + [_resident(w.shape) for w in weights],
        out_specs=blk(d),
        out_shape=jax.ShapeDtypeStruct((bsz, seq, d), F32),
        scratch_shapes=[pltpu.VMEM((conf // LANES, tm + 2 * HALO, LANES), F32), pltpu.VMEM((tm, conf), F32),
                        pltpu.VMEM((tm, conf), BF16), pltpu.VMEM((tm, d_ff), BF16), pltpu.VMEM((tm, d), BF16)],
        compiler_params=pltpu.CompilerParams(dimension_semantics=("parallel", "arbitrary"),
                                             vmem_limit_bytes=VMEM_LIMIT),
        name="tail",
    )(x, y, z, gates, a, a, a, a, a, a, *weights)


def kernel(x, meta_tokens, norm_mix, w_in, conv_dw_w, conv_dw_b, conv_ln_g, conv_ln_b, conv_out_w,
           ssm_conv_w, ssm_conv_b, dt_bias_f, dt_bias_b, a_log_f, a_log_b, ssm_d, ssm_norm_w, ssm_out_w,
           w_o, norm_ffn, w_gate, w_up, w_down, norm_final):
    bsz, seq, d = x.shape
    assert norm_mix.shape[0] == 1 and seq % ROW_TILE == 0 and meta_tokens.shape[0] == N_META
    conf = conv_dw_w.shape[2]
    d_inner = ssm_norm_w.shape[1]
    n_heads = ssm_d.shape[1]
    n_xbc = ssm_conv_w.shape[2]
    assert conf == d

    w = w_in[0].astype(BF16)
    o_dt = 2 * conf + d_inner + n_xbc
    o_g = o_dt + 2 * n_heads
    dt_bias = jnp.concatenate([dt_bias_f[0], dt_bias_b[0]])
    a_log = jnp.concatenate([a_log_f[0], a_log_b[0]])
    meta_chunk = jnp.concatenate([jnp.zeros((META_PAD, d), x.dtype), meta_tokens.astype(x.dtype)], axis=0)

    a, z, xbc, decc, decr, cols, gates = _inproj(
        x, meta_chunk, norm_mix, w, w[:, o_g:], dt_bias[None, :], dt_bias[:, None], a_log[None, :], a_log[:, None],
        ssm_conv_w[0], ssm_conv_b, tm=ROW_TILE, conf=conf, d_inner=d_inner)
    y = _ssd(xbc, decc, decr, cols, jnp.repeat(ssm_d[0], SSM_HEADDIM)[None, :],
             n_x_chunks=seq // CHUNK, d_inner=d_inner)
    tail_weights = [conv_dw_w[0], conv_dw_b, conv_ln_g, conv_ln_b, conv_out_w[0].astype(BF16), ssm_norm_w,
                    ssm_out_w[0].astype(BF16), w_o[0].astype(BF16), norm_ffn, w_gate[0].astype(BF16),
                    w_up[0].astype(BF16), w_down[0].astype(BF16), norm_final[None, :]]
    return _tail(x, y, z, gates, a, tail_weights, tm=ROW_TILE)
```

```python
import functools

import jax
import jax.numpy as jnp
import numpy as np
from jax import lax
from jax.experimental import pallas as pl
from jax.experimental.pallas import tpu as pltpu

F32 = jnp.float32
BF16 = jnp.bfloat16

N_META = 16
CHUNK = 128
META_PAD = CHUNK - N_META
CONF_KERNEL = 31
SSM_CONV = 7
SSM_HEADDIM = 64
SSM_GROUPS = 4
D_STATE = 128
EPS = 1e-6
LANES = 128
MXU_COLS = 2 * LANES
BF16_TILE_ROWS = 16
HALO = BF16_TILE_ROWS
ROW_TILE = 2 * CHUNK
VMEM_LIMIT = 60 * 1024 * 1024


def _sigmoid(v):
    return 1.0 / (1.0 + jnp.exp(-v))


def _silu(v):
    return v * _sigmoid(v)


def _softplus(v):
    return jnp.maximum(v, 0.0) + jnp.log1p(jnp.exp(-jnp.abs(v)))


def _dot(a, b):
    return jnp.dot(a, b, preferred_element_type=F32)


def _split3(v):
    hi = v.astype(BF16)
    rest = v - hi.astype(F32)
    mid = rest.astype(BF16)
    return hi, mid, (rest - mid.astype(F32)).astype(BF16)


def _running_sum_cols(tri, v):
    hi, mid, lo = _split3(v)
    return _dot(tri, hi) + _dot(tri, mid) + _dot(tri, lo)


def _running_sum_rows(v, tri):
    n = v.shape[0]
    r = _dot(jnp.concatenate(_split3(v), axis=0), tri)
    return r[:n] + r[n:2 * n] + r[2 * n:]


def _resident(shape):
    nd = len(shape)
    return pl.BlockSpec(shape, lambda *_: (0,) * nd, pipeline_mode=pl.Buffered(1))


def _dwconv_tile(buf_ref, w_ref, b_ref, ksize, t, rows, row0=0):
    half = (ksize - 1) // 2
    sl = slice(t * LANES, (t + 1) * LANES)
    acc = jnp.broadcast_to(b_ref[:, sl], (rows, LANES))
    for k in range(ksize):
        start = row0 + HALO - half + k
        acc = acc + buf_ref[t, start:start + rows, :] * w_ref[k:k + 1, sl]
    return acc


def _order_after(dst_ref, src):
    part = None
    for r in range(0, src.shape[0], HALO):
        for c in range(0, src.shape[1], LANES):
            piece = src[r:r + HALO, c:c + LANES]
            part = piece if part is None else part + piece
    zero = pltpu.bitcast((pltpu.bitcast(part, jnp.uint32) >> 16) >> 16, F32)
    dst_ref[0:HALO, 0:LANES] = dst_ref[0:HALO, 0:LANES] + zero.astype(dst_ref.dtype)


def _inproj_kernel(prev_ref, main_ref, next_ref, meta_ref, g_ref, w_ref, w_gates_ref, dtb_ref, dtbt_ref,
                   alog_row_ref, alog_col_ref, cw_ref, cb_ref,
                   a_ref, z_ref, xbc_ref, decc_ref, decr_ref, cols_ref, gates_ref, hs_ref, hn_ref, buf_ref,
                   *, n_x_tiles, conf, d_inner):
    tm = main_ref.shape[1]
    n_xbc = xbc_ref.shape[2]
    n_dt = dtb_ref.shape[1]
    o_z = 2 * conf
    o_xbc = o_z + d_inner
    o_dt = o_xbc + n_xbc
    j = pl.program_id(1)

    @pl.when(j < n_x_tiles)
    def _():
        hs_ref[0:HALO, :] = jnp.where(j == 0, meta_ref[META_PAD:CHUNK, :], prev_ref[0])
        hs_ref[HALO:HALO + tm, :] = main_ref[0]
        hs_ref[HALO + tm:, :] = jnp.where(j == n_x_tiles - 1, 0.0, next_ref[0])

    @pl.when(j == n_x_tiles)
    def _():
        hs_ref[0:HALO, :] = jnp.zeros((HALO, hs_ref.shape[1]), F32)
        hs_ref[HALO:HALO + CHUNK, :] = meta_ref[...]
        hs_ref[HALO + CHUNK:HALO + CHUNK + HALO, :] = next_ref[0]
        hs_ref[HALO + CHUNK + HALO:, :] = jnp.zeros((tm - CHUNK, hs_ref.shape[1]), F32)

    x = hs_ref[...]
    ms = jnp.mean(x * x, axis=-1, keepdims=True)
    hn_ref[...] = (x * lax.rsqrt(ms + EPS) * g_ref[...]).astype(BF16)
    rows = slice(HALO, HALO + tm)

    def cols(k, base=0):
        return slice(base + k * MXU_COLS, base + (k + 1) * MXU_COLS)

    def xbc_chunk(k):
        r = _dot(hn_ref[...], w_ref[:, cols(k, o_xbc)])
        buf_ref[2 * k] = r[:, :LANES]
        buf_ref[2 * k + 1] = r[:, LANES:]

    def conv_tile(t):
        xbc_ref[0, :, t * LANES:(t + 1) * LANES] = _silu(
            _dwconv_tile(buf_ref, cw_ref, cb_ref, SSM_CONV, t, tm)).astype(BF16)

    def glu_chunk(k):
        a_ref[0, :, cols(k)] = (_dot(hn_ref[rows, :], w_ref[:, cols(k)])
                                * _sigmoid(_dot(hn_ref[rows, :], w_ref[:, cols(k, conf)]))).astype(BF16)

    def z_chunk(k):
        z_ref[0, :, cols(k)] = _dot(hn_ref[rows, :], w_ref[:, cols(k, o_z)]).astype(BF16)

    def gates_chunk(k):
        gates_ref[0, :, cols(k)] = _sigmoid(_dot(hn_ref[rows, :], w_gates_ref[:, cols(k)])).astype(BF16)

    others = ([functools.partial(glu_chunk, k) for k in range(conf // MXU_COLS)]
              + [functools.partial(z_chunk, k) for k in range(d_inner // MXU_COLS)]
              + [functools.partial(gates_chunk, k) for k in range(gates_ref.shape[2] // MXU_COLS)])

    def decay_terms():
        r = _dot(hn_ref[rows, :], w_ref[:, o_dt:o_dt + LANES])
        dt = _softplus(r[:, :n_dt] + dtb_ref[...])
        dtt = _softplus(r.T[:n_dt, :] + dtbt_ref[...])
        first_live = jnp.where(j == n_x_tiles, META_PAD, 0)
        dt = jnp.where(lax.broadcasted_iota(jnp.int32, (tm, 1), 0) >= first_live, dt, 0.0)
        dtt = jnp.where(lax.broadcasted_iota(jnp.int32, (1, tm), 1) >= first_live, dtt, 0.0)

        n_heads = n_dt // 2
        lower, upper, _, _ = _triangles()
        neg_a_row = -jnp.exp(alog_row_ref[...])
        neg_a_col = -jnp.exp(alog_col_ref[...])
        for c in range(tm // CHUNK):
            rs = slice(c * CHUNK, (c + 1) * CHUNK)
            dt_c, dt_r = dt[rs, :], dtt[:, rs]
            adt_c, adt_r = dt_c * neg_a_row, dt_r * neg_a_col
            cumf = _running_sum_cols(lower, adt_c[:, :n_heads])
            cumb = _running_sum_cols(upper, adt_c[:, n_heads:])
            to_edge_f = jnp.exp(cumf[CHUNK - 1:CHUNK, :] - cumf) * dt_c[:, :n_heads]
            to_edge_b = jnp.exp(cumb[0:1, :] - cumb) * dt_c[:, n_heads:]
            decc = jnp.concatenate([cumf, cumb, dt_c[:, n_heads:], jnp.zeros_like(cumf)], axis=1)
            cols = jnp.concatenate([to_edge_f, jnp.exp(cumf), jnp.exp(cumb), to_edge_b], axis=1)
            decr = jnp.concatenate(
                [_running_sum_rows(adt_r[:n_heads], upper) - jnp.log(dt_r[:n_heads]),
                 _running_sum_rows(adt_r[n_heads:], lower) - jnp.log(dt_r[n_heads:])], axis=0)
            decc_ref[0, rs, :] = decc
            cols_ref[0, rs, :] = cols.astype(BF16)
            decr_ref[0, :, rs] = decr

    n_chunks = n_xbc // MXU_COLS
    xbc_chunk(0)
    for k in range(n_chunks):
        if k + 1 < n_chunks:
            xbc_chunk(k + 1)
        share = (len(others) + n_chunks - 1 - k) // (n_chunks - k)
        for i in range(2):
            for _ in range((share + 1 - i) // 2):
                others.pop(0)()
            conv_tile(2 * k + i)
    decay_terms()


def _inproj(x, meta_chunk, g, w, w_gates, dtb, dtbt, alog_row, alog_col, cw, cb, *, tm, conf, d_inner):
    bsz, seq, d = x.shape
    frame = seq + CHUNK
    n_x_tiles = seq // tm
    per = tm // HALO
    n_xbc, n_dt, n_g = cw.shape[1], dtb.shape[1], w_gates.shape[1]
    last = n_x_tiles - 1
    main_map = lambda b, j: (b, jnp.minimum(j, last), 0)
    prev_map = lambda b, j: (b, jnp.where(jnp.logical_or(j == 0, j > last), 0, j * per - 1), 0)
    next_map = lambda b, j: (b, jnp.where(j >= last, 0, (j + 1) * per), 0)
    out_spec = lambda n: pl.BlockSpec((1, tm, n), lambda b, j: (b, j, 0))
    weights = [g, w, w_gates, dtb, dtbt, alog_row, alog_col, cw, cb]
    return pl.pallas_call(
        functools.partial(_inproj_kernel, n_x_tiles=n_x_tiles, conf=conf, d_inner=d_inner),
        grid=(bsz, n_x_tiles + 1),
        in_specs=[pl.BlockSpec((1, HALO, d), prev_map), pl.BlockSpec((1, tm, d), main_map),
                  pl.BlockSpec((1, HALO, d), next_map), _resident(meta_chunk.shape)]
                 + [_resident(v.shape) for v in weights],
        out_specs=[out_spec(conf), out_spec(d_inner), out_spec(n_xbc), out_spec(2 * n_dt),
                   pl.BlockSpec((1, n_dt, tm), lambda b, j: (b, 0, j)), out_spec(2 * n_dt), out_spec(n_g)],
        out_shape=[jax.ShapeDtypeStruct((bsz, frame, conf), BF16), jax.ShapeDtypeStruct((bsz, frame, d_inner), BF16),
                   jax.ShapeDtypeStruct((bsz, frame, n_xbc), BF16), jax.ShapeDtypeStruct((bsz, frame, 2 * n_dt), F32),
                   jax.ShapeDtypeStruct((bsz, n_dt, frame), F32), jax.ShapeDtypeStruct((bsz, frame, 2 * n_dt), BF16),
                   jax.ShapeDtypeStruct((bsz, frame, n_g), BF16)],
        scratch_shapes=[pltpu.VMEM((tm + 2 * HALO, d), F32), pltpu.VMEM((tm + 2 * HALO, d), BF16),
                        pltpu.VMEM((n_xbc // LANES, tm + 2 * HALO, LANES), F32)],
        compiler_params=pltpu.CompilerParams(dimension_semantics=("parallel", "parallel"),
                                             vmem_limit_bytes=VMEM_LIMIT),
        name="inproj",
    )(x, x, x, meta_chunk, *weights)


def _spread_matrix(n):
    return np.kron(np.eye(n, dtype=np.float32), np.ones((1, SSM_HEADDIM), np.float32))


def _triangles():
    li = lax.broadcasted_iota(jnp.int32, (CHUNK, CHUNK), 0)
    si = lax.broadcasted_iota(jnp.int32, (CHUNK, CHUNK), 1)
    return (li >= si).astype(F32).astype(BF16), (li <= si).astype(F32).astype(BF16), li, si


def _hi_lo_rows(v):
    hi = v.astype(BF16).astype(F32)
    row = lax.broadcasted_iota(jnp.int32, (BF16_TILE_ROWS, v.shape[1]), 0)
    return jnp.where(row == 0, hi, jnp.where(row == 1, v - hi, 0.0)).astype(BF16)


def _exp_lanes(row, lo, hi):
    lane = lax.broadcasted_iota(jnp.int32, row.shape, 1)
    return jnp.exp(jnp.where(jnp.logical_and(lane >= lo, lane < hi), row, 0.0))


def _ssd_bwd_state_kernel(x_ref, b_ref, decc_ref, cols_ref, spread_edge_ref, spread_tot_ref, sb_ref, state_ref,
                          *, n_heads, hpg):
    @pl.when(pl.program_id(1) == 0)
    def _():
        state_ref[...] = jnp.zeros_like(state_ref)

    gp = hpg * SSM_HEADDIM
    chunks = []
    for c in (1, 0):
        rows = slice(c * CHUNK, (c + 1) * CHUNK)
        wide = _dot(cols_ref[0, rows, :], spread_edge_ref[...])
        total = _exp_lanes(decc_ref[0, c * CHUNK:c * CHUNK + 1, :], n_heads, 2 * n_heads)
        chunks.append((c, rows, wide, _dot(_hi_lo_rows(total), spread_tot_ref[...])))
    for g in range(SSM_GROUPS):
        lanes = slice(g * gp, (g + 1) * gp)
        state = state_ref[g]
        for c, rows, wide, tot_wide in chunks:
            weighted = (x_ref[0, rows, lanes].astype(F32) * wide[:, lanes]).astype(BF16)
            s_local = lax.dot_general(b_ref[0, rows, g * D_STATE:(g + 1) * D_STATE], weighted,
                                      (((0,), (0,)), ((), ())), preferred_element_type=F32)
            sb_ref[0, c, g * D_STATE:(g + 1) * D_STATE, :] = state.astype(sb_ref.dtype)
            state = (tot_wide[0:1, lanes] + tot_wide[1:2, lanes]) * state + s_local
        state_ref[g] = state


def _ssd_main_kernel(x_ref, decc_ref, decr_ref, cols_ref, sb_ref, dskip_ref, spread3_ref, spread1_ref,
                     y_ref, sf_ref, *, d_inner, n_heads, hpg):
    @pl.when(pl.program_id(1) == 0)
    def _():
        sf_ref[...] = jnp.zeros_like(sf_ref)

    gp = hpg * SSM_HEADDIM
    _, _, li, si = _triangles()
    fwd_sees = li >= si
    on_diag = li == si
    left_head = lax.broadcasted_iota(jnp.int32, (CHUNK, 2 * SSM_HEADDIM), 1) < SSM_HEADDIM

    def b_of(g):
        return x_ref[0, :, d_inner + g * D_STATE:d_inner + (g + 1) * D_STATE]

    def c_of(g):
        return x_ref[0, :, d_inner + (SSM_GROUPS + g) * D_STATE:d_inner + (SSM_GROUPS + g + 1) * D_STATE]

    decc = decc_ref[0]
    decr = decr_ref[0]
    cbs, diag_cols = [], []
    for g in range(SSM_GROUPS):
        cb = lax.dot_general(c_of(g), b_of(g), (((1,), (1,)), ((), ())), preferred_element_type=F32)
        cbs.append(cb)
        cb_diag = jnp.sum(jnp.where(on_diag, cb, 0.0), axis=1, keepdims=True)
        diag_cols.append(cb_diag * decc[:, 2 * n_heads + g * hpg:2 * n_heads + (g + 1) * hpg])
    wide = _dot(cols_ref[0], spread3_ref[...])
    wide_diag = _dot(jnp.concatenate(diag_cols, axis=1).astype(BF16), spread1_ref[...])
    tot_wide = _dot(_hi_lo_rows(_exp_lanes(decc[CHUNK - 1:CHUNK, :], 0, n_heads)), spread3_ref[:, :d_inner])

    for g in range(SSM_GROUPS):
        lanes = slice(g * gp, (g + 1) * gp)
        wlanes = lambda q: slice(q * d_inner + g * gp, q * d_inner + (g + 1) * gp)
        xf = x_ref[0, :, lanes].astype(F32)
        s_in = sf_ref[g]
        y_off = _dot(c_of(g), jnp.concatenate(
            [s_in.astype(BF16), sb_ref[0, 0, g * D_STATE:(g + 1) * D_STATE, :]], axis=1))
        weighted = (xf * wide[:, wlanes(0)]).astype(BF16)
        s_local = lax.dot_general(b_of(g), weighted, (((0,), (0,)), ((), ())), preferred_element_type=F32)
        sf_ref[g] = (tot_wide[0:1, lanes] + tot_wide[1:2, lanes]) * s_in + s_local
        y = (y_off[:, :gp] * wide[:, wlanes(1)] + y_off[:, gp:] * wide[:, wlanes(2)]
             + xf * (dskip_ref[:, lanes] + wide_diag[:, lanes]))

        pairs = []
        for j in range(hpg // 2):
            ms = []
            for h in (g * hpg + 2 * j, g * hpg + 2 * j + 1):
                hb = n_heads + h
                df = (jnp.broadcast_to(decc[:, h:h + 1], (CHUNK, CHUNK))
                      - jnp.broadcast_to(decr[h:h + 1, :], (CHUNK, CHUNK)))
                db = (jnp.broadcast_to(decc[:, hb:hb + 1], (CHUNK, CHUNK))
                      - jnp.broadcast_to(decr[hb:hb + 1, :], (CHUNK, CHUNK)))
                ms.append((cbs[g] * jnp.exp(jnp.where(fwd_sees, df, db))).astype(BF16))
            xp = x_ref[0, :, g * gp + 2 * j * SSM_HEADDIM:g * gp + (2 * j + 2) * SSM_HEADDIM]
            zero = jnp.zeros_like(xp)
            block_diag = jnp.concatenate([jnp.where(left_head, xp, zero), jnp.where(left_head, zero, xp)], axis=0)
            pairs.append(_dot(jnp.concatenate(ms, axis=1), block_diag))
        y_ref[0, :, lanes] = (y + jnp.concatenate(pairs, axis=1)).astype(y_ref.dtype)


def _ssd(xbc, decc, decr, cols, dskip, *, n_x_chunks, d_inner):
    bsz, frame, c = xbc.shape
    n_heads = decc.shape[2] // 4
    hpg = n_heads // SSM_GROUPS
    gp = hpg * SSM_HEADDIM
    gn = SSM_GROUPS * D_STATE
    nc = n_x_chunks + 1
    assert c == d_inner + 2 * gn and d_inner % gn == 0
    n_pairs = n_x_chunks // 2
    pair_block = lambda shape, col=0: pl.BlockSpec(
        (1,) + shape, lambda b, i: (b, n_pairs - 1 - i) + (col,) + (0,) * (len(shape) - 2))
    params = pltpu.CompilerParams(dimension_semantics=("parallel", "arbitrary"), vmem_limit_bytes=VMEM_LIMIT)
    state_scratch = pltpu.VMEM((SSM_GROUPS, D_STATE, gp), F32)
    spread1 = _spread_matrix(n_heads)
    blank = np.zeros_like(spread1)
    quantity = lambda q: np.concatenate([spread1 if r == q else blank for r in range(4)], axis=0)
    spread3 = jnp.asarray(np.concatenate([quantity(0), quantity(1), quantity(2)], axis=1), BF16)

    sb = pl.pallas_call(
        functools.partial(_ssd_bwd_state_kernel, n_heads=n_heads, hpg=hpg),
        grid=(bsz, n_pairs),
        in_specs=[pair_block((2 * CHUNK, d_inner)), pair_block((2 * CHUNK, gn), d_inner // gn),
                  pair_block((2 * CHUNK, 4 * n_heads)), pair_block((2 * CHUNK, 4 * n_heads)),
                  _resident((4 * n_heads, d_inner)), _resident((4 * n_heads, d_inner))],
        out_specs=pair_block((2, gn, gp)),
        out_shape=jax.ShapeDtypeStruct((bsz, n_x_chunks, gn, gp), BF16),
        scratch_shapes=[state_scratch],
        compiler_params=params,
        name="ssd_bwd_state",
    )(xbc, xbc, decc, cols, jnp.asarray(quantity(3), BF16), jnp.asarray(quantity(1), BF16))

    fwd = lambda i: (i + n_x_chunks) % nc
    row_block = lambda n: pl.BlockSpec((1, CHUNK, n), lambda b, i: (b, fwd(i), 0))
    state_block = pl.BlockSpec((1, 1, gn, gp), lambda b, i: (b, jnp.minimum(fwd(i), n_x_chunks - 1), 0, 0))
    return pl.pallas_call(
        functools.partial(_ssd_main_kernel, d_inner=d_inner, n_heads=n_heads, hpg=hpg),
        grid=(bsz, nc),
        in_specs=[row_block(c), row_block(4 * n_heads),
                  pl.BlockSpec((1, 2 * n_heads, CHUNK), lambda b, i: (b, 0, fwd(i))), row_block(4 * n_heads),
                  state_block, _resident(dskip.shape), _resident(spread3.shape), _resident(spread1.shape)],
        out_specs=row_block(d_inner),
        out_shape=jax.ShapeDtypeStruct((bsz, frame, d_inner), BF16),
        scratch_shapes=[state_scratch],
        compiler_params=params,
        name="ssd_main",
    )(xbc, decc, decr, cols, sb, dskip, spread3, jnp.asarray(spread1, BF16))


def _tail_kernel(x_ref, y_ref, z_ref, gates_ref, a0_prev_ref, a0_ref, a0_next_ref, a_prev_ref, a_ref, a_next_ref,
                 cw_ref, cb_ref, lng_ref, lnb_ref, wco_ref, nw_ref, wso_ref, wo_ref, nffn_ref,
                 wg_ref, wu_ref, wd_ref, nfin_ref, o_ref, buf_ref, conv_ref, h_ref, act_ref, hn_ref):
    tm, d = x_ref.shape[1], x_ref.shape[2]
    d_inner = z_ref.shape[2]
    gc = d_inner // SSM_GROUPS
    n_tiles = buf_ref.shape[0]

    def fill_window(p_ref, m_ref, n_ref):
        for t in range(n_tiles):
            sl = slice(t * LANES, (t + 1) * LANES)
            buf_ref[t, 0:HALO, :] = p_ref[0, :, sl].astype(F32)
            buf_ref[t, HALO:HALO + tm, :] = m_ref[0, :, sl].astype(F32)
            buf_ref[t, HALO + tm:, :] = n_ref[0, :, sl].astype(F32)

    n_units = 2 * n_tiles

    def conv_unit(u):
        t, row0 = u // 2, (u % 2) * (tm // 2)
        acc = _dwconv_tile(buf_ref, cw_ref, cb_ref, CONF_KERNEL, t, tm // 2, row0)
        conv_ref[row0:row0 + tm // 2, t * LANES:(t + 1) * LANES] = acc
        return acc

    def norm_act():
        v = conv_ref[...]
        mu = jnp.mean(v, axis=-1, keepdims=True)
        vc = v - mu
        var = jnp.mean(vc * vc, axis=-1, keepdims=True)
        h = _silu(vc * lax.rsqrt(var + EPS) * lng_ref[...] + lnb_ref[...])
        h_ref[...] = h.astype(BF16)
        return h

    @pl.when(pl.program_id(1) == 0)
    def _():
        fill_window(a0_prev_ref, a0_ref, a0_next_ref)
        for u in range(n_units):
            conv_unit(u)
        norm_act()

    y_conv = _dot(h_ref[...], wco_ref[...])
    fill_window(a_prev_ref, a_ref, a_next_ref)

    v = y_ref[0].astype(F32) * _silu(z_ref[0].astype(F32))
    parts = []
    for g in range(SSM_GROUPS):
        vg = v[:, g * gc:(g + 1) * gc]
        ms = jnp.mean(vg * vg, axis=-1, keepdims=True)
        parts.append((vg * lax.rsqrt(ms + EPS) * nw_ref[:, g * gc:(g + 1) * gc]).astype(BF16))
    y_ssm = _dot(jnp.concatenate(parts, axis=1), wso_ref[...])

    merged = gates_ref[0, :, :d].astype(F32) * y_conv + gates_ref[0, :, d:].astype(F32) * y_ssm
    hs = x_ref[0] + _dot(merged.astype(BF16), wo_ref[...])

    ms = jnp.mean(hs * hs, axis=-1, keepdims=True)
    hn_ref[...] = (hs * lax.rsqrt(ms + EPS) * nffn_ref[...]).astype(BF16)
    n_ffn_chunks = wg_ref.shape[1] // MXU_COLS
    for c in range(n_ffn_chunks):
        sl = slice(c * MXU_COLS, (c + 1) * MXU_COLS)
        hn = hn_ref[...]
        act_ref[:, sl] = (_silu(_dot(hn, wg_ref[:, sl])) * _dot(hn, wu_ref[:, sl])).astype(BF16)
        for u in range(c * n_units // n_ffn_chunks, (c + 1) * n_units // n_ffn_chunks):
            _order_after(hn_ref if c + 1 < n_ffn_chunks else act_ref, conv_unit(u))
    hs = hs + _dot(act_ref[...], wd_ref[...])
    norm_act()

    ms = jnp.mean(hs * hs, axis=-1, keepdims=True)
    o_ref[0] = hs * lax.rsqrt(ms + EPS) * nfin_ref[...]


def _tail(x, y, z, gates, a, weights, *, tm):
    bsz, seq, d = x.shape
    d_inner = y.shape[2]
    conf = a.shape[2]
    d_ff = weights[-4].shape[1]
    per = tm // HALO
    last = seq // tm - 1
    meta_rows_block = (seq + META_PAD) // HALO
    blk = lambda n: pl.BlockSpec((1, tm, n), lambda b, j: (b, j, 0))
    halo = lambda index: pl.BlockSpec((1, HALO, conf), lambda b, j: (b, index(j), 0))
    nxt = lambda j: jnp.minimum(j + 1, last)
    return pl.pallas_call(
        _tail_kernel,
        grid=(bsz, seq // tm),
        in_specs=[blk(d), blk(d_inner), blk(d_inner), blk(2 * d),
                  halo(lambda j: meta_rows_block), pl.BlockSpec((1, tm, conf), lambda b, j: (b, 0, 0)),
                  halo(lambda j: per),
                  halo(lambda j: (j + 1) * per - 1), pl.BlockSpec((1, tm, conf), lambda b, j: (b, nxt(j), 0)),
                  halo(lambda j: (nxt(j) + 1) * per)]
                 + [_resident(w.shape) for w in weights],
        out_specs=blk(d),
        out_shape=jax.ShapeDtypeStruct((bsz, seq, d), F32),
        scratch_shapes=[pltpu.VMEM((conf // LANES, tm + 2 * HALO, LANES), F32), pltpu.VMEM((tm, conf), F32),
                        pltpu.VMEM((tm, conf), BF16), pltpu.VMEM((tm, d_ff), BF16), pltpu.VMEM((tm, d), BF16)],
        compiler_params=pltpu.CompilerParams(dimension_semantics=("parallel", "arbitrary"),
                                             vmem_limit_bytes=VMEM_LIMIT),
        name="tail",
    )(x, y, z, gates, a, a, a, a, a, a, *weights)


def kernel(x, meta_tokens, norm_mix, w_in, conv_dw_w, conv_dw_b, conv_ln_g, conv_ln_b, conv_out_w,
           ssm_conv_w, ssm_conv_b, dt_bias_f, dt_bias_b, a_log_f, a_log_b, ssm_d, ssm_norm_w, ssm_out_w,
           w_o, norm_ffn, w_gate, w_up, w_down, norm_final):
    bsz, seq, d = x.shape
    assert norm_mix.shape[0] == 1 and seq % ROW_TILE == 0 and meta_tokens.shape[0] == N_META
    conf = conv_dw_w.shape[2]
    d_inner = ssm_norm_w.shape[1]
    n_heads = ssm_d.shape[1]
    n_xbc = ssm_conv_w.shape[2]
    assert conf == d

    w = w_in[0].astype(BF16)
    o_dt = 2 * conf + d_inner + n_xbc
    o_g = o_dt + 2 * n_heads
    dt_bias = jnp.concatenate([dt_bias_f[0], dt_bias_b[0]])
    a_log = jnp.concatenate([a_log_f[0], a_log_b[0]])
    meta_chunk = jnp.concatenate([jnp.zeros((META_PAD, d), x.dtype), meta_tokens.astype(x.dtype)], axis=0)

    a, z, xbc, decc, decr, cols, gates = _inproj(
        x, meta_chunk, norm_mix, w, w[:, o_g:], dt_bias[None, :], dt_bias[:, None], a_log[None, :], a_log[:, None],
        ssm_conv_w[0], ssm_conv_b, tm=ROW_TILE, conf=conf, d_inner=d_inner)
    y = _ssd(xbc, decc, decr, cols, jnp.repeat(ssm_d[0], SSM_HEADDIM)[None, :],
             n_x_chunks=seq // CHUNK, d_inner=d_inner)
    tail_weights = [conv_dw_w[0], conv_dw_b, conv_ln_g, conv_ln_b, conv_out_w[0].astype(BF16), ssm_norm_w,
                    ssm_out_w[0].astype(BF16), w_o[0].astype(BF16), norm_ffn, w_gate[0].astype(BF16),
                    w_up[0].astype(BF16), w_down[0].astype(BF16), norm_final[None, :]]
    return _tail(x, y, z, gates, a, tail_weights, tm=ROW_TILE)
```

```python
import functools

import jax
import jax.numpy as jnp
import numpy as np
from jax import lax
from jax.experimental import pallas as pl
from jax.experimental.pallas import tpu as pltpu

F32 = jnp.float32
BF16 = jnp.bfloat16

N_META = 16
CHUNK = 128
META_PAD = CHUNK - N_META
CONF_KERNEL = 31
SSM_CONV = 7
SSM_HEADDIM = 64
SSM_GROUPS = 4
D_STATE = 128
EPS = 1e-6
LANES = 128
MXU_COLS = 2 * LANES
HALO = 16
ROW_TILE = 2 * CHUNK
VMEM_LIMIT = 60 * 1024 * 1024


def _sigmoid(v):
    return 1.0 / (1.0 + jnp.exp(-v))


def _silu(v):
    return v * _sigmoid(v)


def _softplus(v):
    return jnp.maximum(v, 0.0) + jnp.log1p(jnp.exp(-jnp.abs(v)))


def _dot(a, b):
    return jnp.dot(a, b, preferred_element_type=F32)


def _split3(v):
    hi = v.astype(BF16)
    rest = v - hi.astype(F32)
    mid = rest.astype(BF16)
    return hi, mid, (rest - mid.astype(F32)).astype(BF16)


def _running_sum_cols(tri, v):
    hi, mid, lo = _split3(v)
    return _dot(tri, hi) + _dot(tri, mid) + _dot(tri, lo)


def _running_sum_rows(v, tri):
    n = v.shape[0]
    r = _dot(jnp.concatenate(_split3(v), axis=0), tri)
    return r[:n] + r[n:2 * n] + r[2 * n:]


def _resident(shape):
    nd = len(shape)
    return pl.BlockSpec(shape, lambda *_: (0,) * nd, pipeline_mode=pl.Buffered(1))


def _dwconv_tile(buf_ref, w_ref, b_ref, ksize, t, rows, row0=0):
    half = (ksize - 1) // 2
    sl = slice(t * LANES, (t + 1) * LANES)
    acc = jnp.broadcast_to(b_ref[:, sl], (rows, LANES))
    for k in range(ksize):
        start = row0 + HALO - half + k
        acc = acc + buf_ref[t, start:start + rows, :] * w_ref[k:k + 1, sl]
    return acc


def _order_after(dst_ref, src):
    part = None
    for r in range(0, src.shape[0], HALO):
        for c in range(0, src.shape[1], LANES):
            piece = src[r:r + HALO, c:c + LANES]
            part = piece if part is None else part + piece
    zero = pltpu.bitcast((pltpu.bitcast(part, jnp.uint32) >> 16) >> 16, F32)
    dst_ref[0:HALO, 0:LANES] = dst_ref[0:HALO, 0:LANES] + zero.astype(dst_ref.dtype)


def _inproj_kernel(prev_ref, main_ref, next_ref, meta_ref, g_ref, w_ref, w_gates_ref, dtb_ref, dtbt_ref,
                   alog_row_ref, alog_col_ref, cw_ref, cb_ref,
                   a_ref, z_ref, xbc_ref, decc_ref, decr_ref, cols_ref, gates_ref, hs_ref, hn_ref, buf_ref,
                   *, n_x_tiles, conf, d_inner):
    tm = main_ref.shape[1]
    n_xbc = xbc_ref.shape[2]
    n_dt = dtb_ref.shape[1]
    o_z = 2 * conf
    o_xbc = o_z + d_inner
    o_dt = o_xbc + n_xbc
    j = pl.program_id(1)

    @pl.when(j < n_x_tiles)
    def _():
        hs_ref[0:HALO, :] = jnp.where(j == 0, meta_ref[META_PAD:CHUNK, :], prev_ref[0])
        hs_ref[HALO:HALO + tm, :] = main_ref[0]
        hs_ref[HALO + tm:, :] = jnp.where(j == n_x_tiles - 1, 0.0, next_ref[0])

    @pl.when(j == n_x_tiles)
    def _():
        hs_ref[0:HALO, :] = jnp.zeros((HALO, hs_ref.shape[1]), F32)
        hs_ref[HALO:HALO + CHUNK, :] = meta_ref[...]
        hs_ref[HALO + CHUNK:HALO + CHUNK + HALO, :] = next_ref[0]
        hs_ref[HALO + CHUNK + HALO:, :] = jnp.zeros((tm - CHUNK, hs_ref.shape[1]), F32)

    x = hs_ref[...]
    ms = jnp.mean(x * x, axis=-1, keepdims=True)
    hn_ref[...] = (x * lax.rsqrt(ms + EPS) * g_ref[...]).astype(BF16)
    rows = slice(HALO, HALO + tm)

    def cols(k, base=0):
        return slice(base + k * MXU_COLS, base + (k + 1) * MXU_COLS)

    def xbc_chunk(k):
        r = _dot(hn_ref[...], w_ref[:, cols(k, o_xbc)])
        buf_ref[2 * k] = r[:, :LANES]
        buf_ref[2 * k + 1] = r[:, LANES:]

    def conv_tile(t):
        xbc_ref[0, :, t * LANES:(t + 1) * LANES] = _silu(
            _dwconv_tile(buf_ref, cw_ref, cb_ref, SSM_CONV, t, tm)).astype(BF16)

    def glu_chunk(k):
        a_ref[0, :, cols(k)] = (_dot(hn_ref[rows, :], w_ref[:, cols(k)])
                                * _sigmoid(_dot(hn_ref[rows, :], w_ref[:, cols(k, conf)]))).astype(BF16)

    def z_chunk(k):
        z_ref[0, :, cols(k)] = _dot(hn_ref[rows, :], w_ref[:, cols(k, o_z)]).astype(BF16)

    def gates_chunk(k):
        gates_ref[0, :, cols(k)] = _sigmoid(_dot(hn_ref[rows, :], w_gates_ref[:, cols(k)])).astype(BF16)

    others = ([functools.partial(glu_chunk, k) for k in range(conf // MXU_COLS)]
              + [functools.partial(z_chunk, k) for k in range(d_inner // MXU_COLS)]
              + [functools.partial(gates_chunk, k) for k in range(gates_ref.shape[2] // MXU_COLS)])
    def decay_terms():
        r = _dot(hn_ref[rows, :], w_ref[:, o_dt:o_dt + LANES])
        dt = _softplus(r[:, :n_dt] + dtb_ref[...])
        dtt = _softplus(r.T[:n_dt, :] + dtbt_ref[...])
        first_live = jnp.where(j == n_x_tiles, META_PAD, 0)
        dt = jnp.where(lax.broadcasted_iota(jnp.int32, (tm, 1), 0) >= first_live, dt, 0.0)
        dtt = jnp.where(lax.broadcasted_iota(jnp.int32, (1, tm), 1) >= first_live, dtt, 0.0)

        n_heads = n_dt // 2
        lower, upper, _, _ = _triangles()
        neg_a_row = -jnp.exp(alog_row_ref[...])
        neg_a_col = -jnp.exp(alog_col_ref[...])
        for c in range(tm // CHUNK):
            rs = slice(c * CHUNK, (c + 1) * CHUNK)
            dt_c, dt_r = dt[rs, :], dtt[:, rs]
            adt_c, adt_r = dt_c * neg_a_row, dt_r * neg_a_col
            cumf = _running_sum_cols(lower, adt_c[:, :n_heads])
            cumb = _running_sum_cols(upper, adt_c[:, n_heads:])
            to_edge_f = jnp.exp(cumf[CHUNK - 1:CHUNK, :] - cumf) * dt_c[:, :n_heads]
            to_edge_b = jnp.exp(cumb[0:1, :] - cumb) * dt_c[:, n_heads:]
            decc = jnp.concatenate([cumf, cumb, dt_c[:, n_heads:], jnp.zeros_like(cumf)], axis=1)
            cols = jnp.concatenate([to_edge_f, jnp.exp(cumf), jnp.exp(cumb), to_edge_b], axis=1)
            decr = jnp.concatenate(
                [_running_sum_rows(adt_r[:n_heads], upper) - jnp.log(dt_r[:n_heads]),
                 _running_sum_rows(adt_r[n_heads:], lower) - jnp.log(dt_r[n_heads:]),
                 jnp.log(dt_r[:n_heads] + dt_r[n_heads:])], axis=0)
            decc_ref[0, rs, :] = decc
            cols_ref[0, rs, :] = cols.astype(BF16)
            decr_ref[0, :, rs] = decr

    n_chunks = n_xbc // MXU_COLS
    xbc_chunk(0)
    for k in range(n_chunks):
        if k + 1 < n_chunks:
            xbc_chunk(k + 1)
        share = (len(others) + n_chunks - 1 - k) // (n_chunks - k)
        for i in range(2):
            for _ in range((share + 1 - i) // 2):
                others.pop(0)()
            conv_tile(2 * k + i)
    decay_terms()


def _inproj(x, meta_chunk, g, w, w_gates, dtb, dtbt, alog_row, alog_col, cw, cb, *, tm, conf, d_inner):
    bsz, seq, d = x.shape
    frame = seq + CHUNK
    n_x_tiles = seq // tm
    per = tm // HALO
    n_xbc, n_dt, n_g = cw.shape[1], dtb.shape[1], w_gates.shape[1]
    last = n_x_tiles - 1
    main_map = lambda b, j: (b, jnp.minimum(j, last), 0)
    prev_map = lambda b, j: (b, jnp.where(jnp.logical_or(j == 0, j > last), 0, j * per - 1), 0)
    next_map = lambda b, j: (b, jnp.where(j >= last, 0, (j + 1) * per), 0)
    out_spec = lambda n: pl.BlockSpec((1, tm, n), lambda b, j: (b, j, 0))
    weights = [g, w, w_gates, dtb, dtbt, alog_row, alog_col, cw, cb]
    return pl.pallas_call(
        functools.partial(_inproj_kernel, n_x_tiles=n_x_tiles, conf=conf, d_inner=d_inner),
        grid=(bsz, n_x_tiles + 1),
        in_specs=[pl.BlockSpec((1, HALO, d), prev_map), pl.BlockSpec((1, tm, d), main_map),
                  pl.BlockSpec((1, HALO, d), next_map), _resident(meta_chunk.shape)]
                 + [_resident(v.shape) for v in weights],
        out_specs=[out_spec(conf), out_spec(d_inner), out_spec(n_xbc), out_spec(2 * n_dt),
                   pl.BlockSpec((1, 3 * n_dt // 2, tm), lambda b, j: (b, 0, j)), out_spec(2 * n_dt), out_spec(n_g)],
        out_shape=[jax.ShapeDtypeStruct((bsz, frame, conf), BF16), jax.ShapeDtypeStruct((bsz, frame, d_inner), BF16),
                   jax.ShapeDtypeStruct((bsz, frame, n_xbc), BF16), jax.ShapeDtypeStruct((bsz, frame, 2 * n_dt), F32),
                   jax.ShapeDtypeStruct((bsz, 3 * n_dt // 2, frame), F32),
                   jax.ShapeDtypeStruct((bsz, frame, 2 * n_dt), BF16),
                   jax.ShapeDtypeStruct((bsz, frame, n_g), BF16)],
        scratch_shapes=[pltpu.VMEM((tm + 2 * HALO, d), F32), pltpu.VMEM((tm + 2 * HALO, d), BF16),
                        pltpu.VMEM((n_xbc // LANES, tm + 2 * HALO, LANES), F32)],
        compiler_params=pltpu.CompilerParams(dimension_semantics=("parallel", "parallel"),
                                             vmem_limit_bytes=VMEM_LIMIT),
        name="inproj",
    )(x, x, x, meta_chunk, *weights)


def _spread_matrix(n):
    return np.kron(np.eye(n, dtype=np.float32), np.ones((1, SSM_HEADDIM), np.float32))


def _triangles():
    li = lax.broadcasted_iota(jnp.int32, (CHUNK, CHUNK), 0)
    si = lax.broadcasted_iota(jnp.int32, (CHUNK, CHUNK), 1)
    return (li >= si).astype(F32).astype(BF16), (li <= si).astype(F32).astype(BF16), li, si


def _hi_lo_rows(v):
    hi = v.astype(BF16).astype(F32)
    row = lax.broadcasted_iota(jnp.int32, (16, v.shape[1]), 0)
    return jnp.where(row == 0, hi, jnp.where(row == 1, v - hi, 0.0)).astype(BF16)


def _exp_lanes(row, lo, hi):
    lane = lax.broadcasted_iota(jnp.int32, row.shape, 1)
    return jnp.exp(jnp.where(jnp.logical_and(lane >= lo, lane < hi), row, 0.0))


def _ssd_bwd_state_kernel(x_ref, b_ref, decc_ref, cols_ref, spread_edge_ref, spread_tot_ref, sb_ref, state_ref,
                          *, n_heads, hpg):
    @pl.when(pl.program_id(1) == 0)
    def _():
        state_ref[...] = jnp.zeros_like(state_ref)

    gp = hpg * SSM_HEADDIM
    chunks = []
    for c in (1, 0):
        rows = slice(c * CHUNK, (c + 1) * CHUNK)
        wide = _dot(cols_ref[0, rows, :], spread_edge_ref[...])
        total = _exp_lanes(decc_ref[0, c * CHUNK:c * CHUNK + 1, :], n_heads, 2 * n_heads)
        chunks.append((c, rows, wide, _dot(_hi_lo_rows(total), spread_tot_ref[...])))
    for g in range(SSM_GROUPS):
        lanes = slice(g * gp, (g + 1) * gp)
        state = state_ref[g]
        for c, rows, wide, tot_wide in chunks:
            weighted = (x_ref[0, rows, lanes].astype(F32) * wide[:, lanes]).astype(BF16)
            s_local = lax.dot_general(b_ref[0, rows, g * D_STATE:(g + 1) * D_STATE], weighted,
                                      (((0,), (0,)), ((), ())), preferred_element_type=F32)
            sb_ref[0, c, g * D_STATE:(g + 1) * D_STATE, :] = state.astype(sb_ref.dtype)
            state = (tot_wide[0:1, lanes] + tot_wide[1:2, lanes]) * state + s_local
        state_ref[g] = state


def _ssd_main_kernel(x_ref, decc_ref, decr_ref, cols_ref, sb_ref, dskip_ref, spread3_ref,
                     y_ref, sf_ref, *, d_inner, n_heads, hpg):
    @pl.when(pl.program_id(1) == 0)
    def _():
        sf_ref[...] = jnp.zeros_like(sf_ref)

    gp = hpg * SSM_HEADDIM
    _, _, li, si = _triangles()
    fwd_sees = li > si
    bwd_sees = li < si
    left_head = lax.broadcasted_iota(jnp.int32, (CHUNK, 2 * SSM_HEADDIM), 1) < SSM_HEADDIM

    def b_of(g):
        return x_ref[0, :, d_inner + g * D_STATE:d_inner + (g + 1) * D_STATE]

    def c_of(g):
        return x_ref[0, :, d_inner + (SSM_GROUPS + g) * D_STATE:d_inner + (SSM_GROUPS + g + 1) * D_STATE]

    decc = decc_ref[0]
    decr = decr_ref[0]
    cbs = [lax.dot_general(c_of(g), b_of(g), (((1,), (1,)), ((), ())), preferred_element_type=F32)
           for g in range(SSM_GROUPS)]
    wide = _dot(cols_ref[0], spread3_ref[...])
    tot_wide = _dot(_hi_lo_rows(_exp_lanes(decc[CHUNK - 1:CHUNK, :], 0, n_heads)), spread3_ref[:, :d_inner])

    for g in range(SSM_GROUPS):
        lanes = slice(g * gp, (g + 1) * gp)
        wlanes = lambda q: slice(q * d_inner + g * gp, q * d_inner + (g + 1) * gp)
        xf = x_ref[0, :, lanes].astype(F32)
        s_in = sf_ref[g]
        y_off = _dot(c_of(g), jnp.concatenate(
            [s_in.astype(BF16), sb_ref[0, 0, g * D_STATE:(g + 1) * D_STATE, :]], axis=1))
        weighted = (xf * wide[:, wlanes(0)]).astype(BF16)
        s_local = lax.dot_general(b_of(g), weighted, (((0,), (0,)), ((), ())), preferred_element_type=F32)
        sf_ref[g] = (tot_wide[0:1, lanes] + tot_wide[1:2, lanes]) * s_in + s_local
        y = (y_off[:, :gp] * wide[:, wlanes(1)] + y_off[:, gp:] * wide[:, wlanes(2)]
             + xf * dskip_ref[:, lanes])

        pairs = []
        for j in range(hpg // 2):
            ms = []
            for h in (g * hpg + 2 * j, g * hpg + 2 * j + 1):
                hb, hd = n_heads + h, 2 * n_heads + h
                df = (jnp.broadcast_to(decc[:, h:h + 1], (CHUNK, CHUNK))
                      - jnp.broadcast_to(decr[h:h + 1, :], (CHUNK, CHUNK)))
                db = (jnp.broadcast_to(decc[:, hb:hb + 1], (CHUNK, CHUNK))
                      - jnp.broadcast_to(decr[hb:hb + 1, :], (CHUNK, CHUNK)))
                dd = jnp.broadcast_to(decr[hd:hd + 1, :], (CHUNK, CHUNK))
                ms.append((cbs[g] * jnp.exp(jnp.where(fwd_sees, df, jnp.where(bwd_sees, db, dd)))).astype(BF16))
            xp = x_ref[0, :, g * gp + 2 * j * SSM_HEADDIM:g * gp + (2 * j + 2) * SSM_HEADDIM]
            zero = jnp.zeros_like(xp)
            block_diag = jnp.concatenate([jnp.where(left_head, xp, zero), jnp.where(left_head, zero, xp)], axis=0)
            pairs.append(_dot(jnp.concatenate(ms, axis=1), block_diag))
        y_ref[0, :, lanes] = (y + jnp.concatenate(pairs, axis=1)).astype(y_ref.dtype)


def _ssd(xbc, decc, decr, cols, dskip, *, n_x_chunks, d_inner):
    bsz, frame, c = xbc.shape
    n_heads = decc.shape[2] // 4
    hpg = n_heads // SSM_GROUPS
    gp = hpg * SSM_HEADDIM
    gn = SSM_GROUPS * D_STATE
    nc = n_x_chunks + 1
    assert c == d_inner + 2 * gn and d_inner % gn == 0
    n_pairs = n_x_chunks // 2
    pair_block = lambda shape, col=0: pl.BlockSpec(
        (1,) + shape, lambda b, i: (b, n_pairs - 1 - i) + (col,) + (0,) * (len(shape) - 2))
    params = pltpu.CompilerParams(dimension_semantics=("parallel", "arbitrary"), vmem_limit_bytes=VMEM_LIMIT)
    state_scratch = pltpu.VMEM((SSM_GROUPS, D_STATE, gp), F32)
    spread1 = _spread_matrix(n_heads)
    blank = np.zeros_like(spread1)
    quantity = lambda q: np.concatenate([spread1 if r == q else blank for r in range(4)], axis=0)
    spread3 = jnp.asarray(np.concatenate([quantity(0), quantity(1), quantity(2)], axis=1), BF16)

    sb = pl.pallas_call(
        functools.partial(_ssd_bwd_state_kernel, n_heads=n_heads, hpg=hpg),
        grid=(bsz, n_pairs),
        in_specs=[pair_block((2 * CHUNK, d_inner)), pair_block((2 * CHUNK, gn), d_inner // gn),
                  pair_block((2 * CHUNK, 4 * n_heads)), pair_block((2 * CHUNK, 4 * n_heads)),
                  _resident((4 * n_heads, d_inner)), _resident((4 * n_heads, d_inner))],
        out_specs=pair_block((2, gn, gp)),
        out_shape=jax.ShapeDtypeStruct((bsz, n_x_chunks, gn, gp), BF16),
        scratch_shapes=[state_scratch],
        compiler_params=params,
        name="ssd_bwd_state",
    )(xbc, xbc, decc, cols, jnp.asarray(quantity(3), BF16), jnp.asarray(quantity(1), BF16))

    fwd = lambda i: (i + n_x_chunks) % nc
    row_block = lambda n: pl.BlockSpec((1, CHUNK, n), lambda b, i: (b, fwd(i), 0))
    state_block = pl.BlockSpec((1, 1, gn, gp), lambda b, i: (b, jnp.minimum(fwd(i), n_x_chunks - 1), 0, 0))
    return pl.pallas_call(
        functools.partial(_ssd_main_kernel, d_inner=d_inner, n_heads=n_heads, hpg=hpg),
        grid=(bsz, nc),
        in_specs=[row_block(c), row_block(4 * n_heads),
                  pl.BlockSpec((1, 3 * n_heads, CHUNK), lambda b, i: (b, 0, fwd(i))), row_block(4 * n_heads),
                  state_block, _resident(dskip.shape), _resident(spread3.shape)],
        out_specs=row_block(d_inner),
        out_shape=jax.ShapeDtypeStruct((bsz, frame, d_inner), BF16),
        scratch_shapes=[state_scratch],
        compiler_params=params,
        name="ssd_main",
    )(xbc, decc, decr, cols, sb, dskip, spread3)


def _tail_kernel(x_ref, y_ref, z_ref, gates_ref, a0_prev_ref, a0_ref, a0_next_ref, a_prev_ref, a_ref, a_next_ref,
                 cw_ref, cb_ref, lng_ref, lnb_ref, wco_ref, nw_ref, wso_ref, wo_ref, nffn_ref,
                 wg_ref, wu_ref, wd_ref, nfin_ref, o_ref, buf_ref, conv_ref, h_ref, act_ref, hn_ref):
    tm, d = x_ref.shape[1], x_ref.shape[2]
    d_inner = z_ref.shape[2]
    gc = d_inner // SSM_GROUPS
    n_tiles = buf_ref.shape[0]

    def fill_window(p_ref, m_ref, n_ref):
        for t in range(n_tiles):
            sl = slice(t * LANES, (t + 1) * LANES)
            buf_ref[t, 0:HALO, :] = p_ref[0, :, sl].astype(F32)
            buf_ref[t, HALO:HALO + tm, :] = m_ref[0, :, sl].astype(F32)
            buf_ref[t, HALO + tm:, :] = n_ref[0, :, sl].astype(F32)

    n_units = 2 * n_tiles

    def conv_unit(u):
        t, row0 = u // 2, (u % 2) * (tm // 2)
        acc = _dwconv_tile(buf_ref, cw_ref, cb_ref, CONF_KERNEL, t, tm // 2, row0)
        conv_ref[row0:row0 + tm // 2, t * LANES:(t + 1) * LANES] = acc
        return acc

    def norm_act():
        v = conv_ref[...]
        mu = jnp.mean(v, axis=-1, keepdims=True)
        vc = v - mu
        var = jnp.mean(vc * vc, axis=-1, keepdims=True)
        h = _silu(vc * lax.rsqrt(var + EPS) * lng_ref[...] + lnb_ref[...])
        h_ref[...] = h.astype(BF16)
        return h

    @pl.when(pl.program_id(1) == 0)
    def _():
        fill_window(a0_prev_ref, a0_ref, a0_next_ref)
        for u in range(n_units):
            conv_unit(u)
        norm_act()

    y_conv = _dot(h_ref[...], wco_ref[...])
    fill_window(a_prev_ref, a_ref, a_next_ref)

    v = y_ref[0].astype(F32) * _silu(z_ref[0].astype(F32))
    parts = []
    for g in range(SSM_GROUPS):
        vg = v[:, g * gc:(g + 1) * gc]
        ms = jnp.mean(vg * vg, axis=-1, keepdims=True)
        parts.append((vg * lax.rsqrt(ms + EPS) * nw_ref[:, g * gc:(g + 1) * gc]).astype(BF16))
    y_ssm = _dot(jnp.concatenate(parts, axis=1), wso_ref[...])

    merged = gates_ref[0, :, :d].astype(F32) * y_conv + gates_ref[0, :, d:].astype(F32) * y_ssm
    hs = x_ref[0] + _dot(merged.astype(BF16), wo_ref[...])

    ms = jnp.mean(hs * hs, axis=-1, keepdims=True)
    hn_ref[...] = (hs * lax.rsqrt(ms + EPS) * nffn_ref[...]).astype(BF16)
    n_ffn_chunks = wg_ref.shape[1] // MXU_COLS
    for c in range(n_ffn_chunks):
        sl = slice(c * MXU_COLS, (c + 1) * MXU_COLS)
        hn = hn_ref[...]
        act_ref[:, sl] = (_silu(_dot(hn, wg_ref[:, sl])) * _dot(hn, wu_ref[:, sl])).astype(BF16)
        for u in range(c * n_units // n_ffn_chunks, (c + 1) * n_units // n_ffn_chunks):
            _order_after(hn_ref if c + 1 < n_ffn_chunks else act_ref, conv_unit(u))
    hs = hs + _dot(act_ref[...], wd_ref[...])
    norm_act()

    ms = jnp.mean(hs * hs, axis=-1, keepdims=True)
    o_ref[0] = hs * lax.rsqrt(ms + EPS) * nfin_ref[...]


def _tail(x, y, z, gates, a, weights, *, tm):
    bsz, seq, d = x.shape
    d_inner = y.shape[2]
    conf = a.shape[2]
    d_ff = weights[-4].shape[1]
    per = tm // HALO
    last = seq // tm - 1
    meta_rows_block = (seq + META_PAD) // HALO
    blk = lambda n: pl.BlockSpec((1, tm, n), lambda b, j: (b, j, 0))
    halo = lambda index: pl.BlockSpec((1, HALO, conf), lambda b, j: (b, index(j), 0))
    nxt = lambda j: jnp.minimum(j + 1, last)
    return pl.pallas_call(
        _tail_kernel,
        grid=(bsz, seq // tm),
        in_specs=[blk(d), blk(d_inner), blk(d_inner), blk(2 * d),
                  halo(lambda j: meta_rows_block), pl.BlockSpec((1, tm, conf), lambda b, j: (b, 0, 0)),
                  halo(lambda j: per),
                  halo(lambda j: (j + 1) * per - 1), pl.BlockSpec((1, tm, conf), lambda b, j: (b, nxt(j), 0)),
                  halo(lambda j: (nxt(j) + 1) * per)]
                 + [_resident(w.shape) for w in weights],
        out_specs=blk(d),
        out_shape=jax.ShapeDtypeStruct((bsz, seq, d), F32),
        scratch_shapes=[pltpu.VMEM((conf // LANES, tm + 2 * HALO, LANES), F32), pltpu.VMEM((tm, conf), F32),
                        pltpu.VMEM((tm, conf), BF16), pltpu.VMEM((tm, d_ff), BF16), pltpu.VMEM((tm, d), BF16)],
        compiler_params=pltpu.CompilerParams(dimension_semantics=("parallel", "arbitrary"),
                                             vmem_limit_bytes=VMEM_LIMIT),
        name="tail",
    )(x, y, z, gates, a, a, a, a, a, a, *weights)


def kernel(x, meta_tokens, norm_mix, w_in, conv_dw_w, conv_dw_b, conv_ln_g, conv_ln_b, conv_out_w,
           ssm_conv_w, ssm_conv_b, dt_bias_f, dt_bias_b, a_log_f, a_log_b, ssm_d, ssm_norm_w, ssm_out_w,
           w_o, norm_ffn, w_gate, w_up, w_down, norm_final):
    bsz, seq, d = x.shape
    assert norm_mix.shape[0] == 1 and seq % ROW_TILE == 0 and meta_tokens.shape[0] == N_META
    conf = conv_dw_w.shape[2]
    d_inner = ssm_norm_w.shape[1]
    n_heads = ssm_d.shape[1]
    n_xbc = ssm_conv_w.shape[2]
    assert conf == d

    w = w_in[0].astype(BF16)
    o_dt = 2 * conf + d_inner + n_xbc
    o_g = o_dt + 2 * n_heads
    dt_bias = jnp.concatenate([dt_bias_f[0], dt_bias_b[0]])
    a_log = jnp.concatenate([a_log_f[0], a_log_b[0]])
    meta_chunk = jnp.concatenate([jnp.zeros((META_PAD, d), x.dtype), meta_tokens.astype(x.dtype)], axis=0)

    a, z, xbc, decc, decr, cols, gates = _inproj(
        x, meta_chunk, norm_mix, w, w[:, o_g:], dt_bias[None, :], dt_bias[:, None], a_log[None, :], a_log[:, None],
        ssm_conv_w[0], ssm_conv_b, tm=ROW_TILE, conf=conf, d_inner=d_inner)
    y = _ssd(xbc, decc, decr, cols, jnp.repeat(ssm_d[0], SSM_HEADDIM)[None, :],
             n_x_chunks=seq // CHUNK, d_inner=d_inner)
    tail_weights = [conv_dw_w[0], conv_dw_b, conv_ln_g, conv_ln_b, conv_out_w[0].astype(BF16), ssm_norm_w,
                    ssm_out_w[0].astype(BF16), w_o[0].astype(BF16), norm_ffn, w_gate[0].astype(BF16),
                    w_up[0].astype(BF16), w_down[0].astype(BF16), norm_final[None, :]]
    return _tail(x, y, z, gates, a, tail_weights, tm=ROW_TILE)
```

```python
import functools

import jax
import jax.numpy as jnp
import numpy as np
from jax import lax
from jax.experimental import pallas as pl
from jax.experimental.pallas import tpu as pltpu

F32 = jnp.float32
BF16 = jnp.bfloat16

N_META = 16
CHUNK = 128
META_PAD = CHUNK - N_META
CONF_KERNEL = 31
SSM_CONV = 7
SSM_HEADDIM = 64
SSM_GROUPS = 4
D_STATE = 128
EPS = 1e-6
LANES = 128
MXU_COLS = 2 * LANES
HALO = 16
ROW_TILE = 2 * CHUNK
VMEM_LIMIT = 63 * 1024 * 1024


def _sigmoid(v):
    return 1.0 / (1.0 + jnp.exp(-v))


def _silu(v):
    return v * _sigmoid(v)


def _softplus(v):
    return jnp.maximum(v, 0.0) + jnp.log1p(jnp.exp(-jnp.abs(v)))


def _dot(a, b):
    return jnp.dot(a, b, preferred_element_type=F32)


def _split3(v):
    hi = v.astype(BF16)
    rest = v - hi.astype(F32)
    mid = rest.astype(BF16)
    return hi, mid, (rest - mid.astype(F32)).astype(BF16)


def _running_sum_cols(tri, v):
    hi, mid, lo = _split3(v)
    return _dot(tri, hi) + _dot(tri, mid) + _dot(tri, lo)


def _running_sum_rows(v, tri):
    n = v.shape[0]
    r = _dot(jnp.concatenate(_split3(v), axis=0), tri)
    return r[:n] + r[n:2 * n] + r[2 * n:]


def _resident(shape):
    nd = len(shape)
    return pl.BlockSpec(shape, lambda *_: (0,) * nd, pipeline_mode=pl.Buffered(1))


def _dwconv_tile(buf_ref, w_ref, b_ref, ksize, t, rows, row0=0):
    half = (ksize - 1) // 2
    sl = slice(t * LANES, (t + 1) * LANES)
    acc = jnp.broadcast_to(b_ref[:, sl], (rows, LANES))
    for k in range(ksize):
        start = row0 + HALO - half + k
        acc = acc + buf_ref[t, start:start + rows, :] * w_ref[k:k + 1, sl]
    return acc


def _order_after(dst_ref, src):
    part = None
    for r in range(0, src.shape[0], HALO):
        for c in range(0, src.shape[1], LANES):
            piece = src[r:r + HALO, c:c + LANES]
            part = piece if part is None else part + piece
    zero = pltpu.bitcast((pltpu.bitcast(part, jnp.uint32) >> 16) >> 16, F32)
    dst_ref[0:HALO, 0:LANES] = dst_ref[0:HALO, 0:LANES] + zero.astype(dst_ref.dtype)


def _inproj_kernel(prev_ref, main_ref, next_ref, meta_ref, g_ref, w_ref, w_gates_ref, dtb_ref, dtbt_ref,
                   alog_row_ref, alog_col_ref, cw_ref, cb_ref,
                   a_ref, z_ref, xbc_ref, decc_ref, decr_ref, cols_ref, gates_ref, hs_ref, hn_ref, buf_ref,
                   *, n_x_tiles, conf, d_inner):
    tm = main_ref.shape[1]
    n_xbc = xbc_ref.shape[2]
    n_dt = dtb_ref.shape[1]
    o_z = 2 * conf
    o_xbc = o_z + d_inner
    o_dt = o_xbc + n_xbc
    j = pl.program_id(1)

    @pl.when(j < n_x_tiles)
    def _():
        hs_ref[0:HALO, :] = jnp.where(j == 0, meta_ref[META_PAD:CHUNK, :], prev_ref[0])
        hs_ref[HALO:HALO + tm, :] = main_ref[0]
        hs_ref[HALO + tm:, :] = jnp.where(j == n_x_tiles - 1, 0.0, next_ref[0])

    @pl.when(j == n_x_tiles)
    def _():
        hs_ref[0:HALO, :] = jnp.zeros((HALO, hs_ref.shape[1]), F32)
        hs_ref[HALO:HALO + CHUNK, :] = meta_ref[...]
        hs_ref[HALO + CHUNK:HALO + CHUNK + HALO, :] = next_ref[0]
        hs_ref[HALO + CHUNK + HALO:, :] = jnp.zeros((tm - CHUNK, hs_ref.shape[1]), F32)

    x = hs_ref[...]
    ms = jnp.mean(x * x, axis=-1, keepdims=True)
    hn_ref[...] = (x * lax.rsqrt(ms + EPS) * g_ref[...]).astype(BF16)
    rows = slice(HALO, HALO + tm)

    def cols(k, base=0):
        return slice(base + k * MXU_COLS, base + (k + 1) * MXU_COLS)

    def xbc_chunk(k):
        r = _dot(hn_ref[...], w_ref[:, cols(k, o_xbc)])
        buf_ref[2 * k] = r[:, :LANES]
        buf_ref[2 * k + 1] = r[:, LANES:]

    def conv_tile(t):
        xbc_ref[0, :, t * LANES:(t + 1) * LANES] = _silu(
            _dwconv_tile(buf_ref, cw_ref, cb_ref, SSM_CONV, t, tm)).astype(BF16)

    def glu_chunk(k):
        a_ref[0, :, cols(k)] = (_dot(hn_ref[rows, :], w_ref[:, cols(k)])
                                * _sigmoid(_dot(hn_ref[rows, :], w_ref[:, cols(k, conf)]))).astype(BF16)

    def z_chunk(k):
        z_ref[0, :, cols(k)] = _dot(hn_ref[rows, :], w_ref[:, cols(k, o_z)]).astype(BF16)

    def gates_chunk(k):
        gates_ref[0, :, cols(k)] = _sigmoid(_dot(hn_ref[rows, :], w_gates_ref[:, cols(k)])).astype(BF16)

    others = ([functools.partial(glu_chunk, k) for k in range(conf // MXU_COLS)]
              + [functools.partial(z_chunk, k) for k in range(d_inner // MXU_COLS)]
              + [functools.partial(gates_chunk, k) for k in range(gates_ref.shape[2] // MXU_COLS)])
    def decay_terms():
        r = _dot(hn_ref[rows, :], w_ref[:, o_dt:o_dt + LANES])
        dt = _softplus(r[:, :n_dt] + dtb_ref[...])
        dtt = _softplus(r.T[:n_dt, :] + dtbt_ref[...])
        first_live = jnp.where(j == n_x_tiles, META_PAD, 0)
        dt = jnp.where(lax.broadcasted_iota(jnp.int32, (tm, 1), 0) >= first_live, dt, 0.0)
        dtt = jnp.where(lax.broadcasted_iota(jnp.int32, (1, tm), 1) >= first_live, dtt, 0.0)

        n_heads = n_dt // 2
        lower, upper, _, _ = _triangles()
        neg_a_row = -jnp.exp(alog_row_ref[...])
        neg_a_col = -jnp.exp(alog_col_ref[...])
        for c in range(tm // CHUNK):
            rs = slice(c * CHUNK, (c + 1) * CHUNK)
            dt_c, dt_r = dt[rs, :], dtt[:, rs]
            adt_c, adt_r = dt_c * neg_a_row, dt_r * neg_a_col
            cumf = _running_sum_cols(lower, adt_c[:, :n_heads])
            cumb = _running_sum_cols(upper, adt_c[:, n_heads:])
            to_edge_f = jnp.exp(cumf[CHUNK - 1:CHUNK, :] - cumf) * dt_c[:, :n_heads]
            to_edge_b = jnp.exp(cumb[0:1, :] - cumb) * dt_c[:, n_heads:]
            decc = jnp.concatenate([cumf, cumb, dt_c[:, n_heads:], jnp.zeros_like(cumf)], axis=1)
            cols = jnp.concatenate([to_edge_f, jnp.exp(cumf), jnp.exp(cumb), to_edge_b], axis=1)
            decr = jnp.concatenate(
                [_running_sum_rows(adt_r[:n_heads], upper) - jnp.log(dt_r[:n_heads]),
                 _running_sum_rows(adt_r[n_heads:], lower) - jnp.log(dt_r[n_heads:]),
                 jnp.log(dt_r[:n_heads] + dt_r[n_heads:])], axis=0)
            decc_ref[0, rs, :] = decc
            cols_ref[0, rs, :] = cols.astype(BF16)
            decr_ref[0, :, rs] = decr

    n_chunks = n_xbc // MXU_COLS
    xbc_chunk(0)
    for k in range(n_chunks):
        if k + 1 < n_chunks:
            xbc_chunk(k + 1)
        share = (len(others) + n_chunks - 1 - k) // (n_chunks - k)
        for i in range(2):
            for _ in range((share + 1 - i) // 2):
                others.pop(0)()
            conv_tile(2 * k + i)
    decay_terms()


def _inproj(x, meta_chunk, g, w, w_gates, dtb, dtbt, alog_row, alog_col, cw, cb, *, tm, conf, d_inner):
    bsz, seq, d = x.shape
    frame = seq + CHUNK
    n_x_tiles = seq // tm
    per = tm // HALO
    n_xbc, n_dt, n_g = cw.shape[1], dtb.shape[1], w_gates.shape[1]
    last = n_x_tiles - 1
    main_map = lambda b, j: (b, jnp.minimum(j, last), 0)
    prev_map = lambda b, j: (b, jnp.where(jnp.logical_or(j == 0, j > last), 0, j * per - 1), 0)
    next_map = lambda b, j: (b, jnp.where(j >= last, 0, (j + 1) * per), 0)
    out_spec = lambda n: pl.BlockSpec((1, tm, n), lambda b, j: (b, j, 0))
    weights = [g, w, w_gates, dtb, dtbt, alog_row, alog_col, cw, cb]
    return pl.pallas_call(
        functools.partial(_inproj_kernel, n_x_tiles=n_x_tiles, conf=conf, d_inner=d_inner),
        grid=(bsz, n_x_tiles + 1),
        in_specs=[pl.BlockSpec((1, HALO, d), prev_map), pl.BlockSpec((1, tm, d), main_map),
                  pl.BlockSpec((1, HALO, d), next_map), _resident(meta_chunk.shape)]
                 + [_resident(v.shape) for v in weights],
        out_specs=[out_spec(conf), out_spec(d_inner), out_spec(n_xbc), out_spec(2 * n_dt),
                   pl.BlockSpec((1, 3 * n_dt // 2, tm), lambda b, j: (b, 0, j)), out_spec(2 * n_dt), out_spec(n_g)],
        out_shape=[jax.ShapeDtypeStruct((bsz, frame, conf), BF16), jax.ShapeDtypeStruct((bsz, frame, d_inner), BF16),
                   jax.ShapeDtypeStruct((bsz, frame, n_xbc), BF16), jax.ShapeDtypeStruct((bsz, frame, 2 * n_dt), F32),
                   jax.ShapeDtypeStruct((bsz, 3 * n_dt // 2, frame), F32),
                   jax.ShapeDtypeStruct((bsz, frame, 2 * n_dt), BF16),
                   jax.ShapeDtypeStruct((bsz, frame, n_g), BF16)],
        scratch_shapes=[pltpu.VMEM((tm + 2 * HALO, d), F32), pltpu.VMEM((tm + 2 * HALO, d), BF16),
                        pltpu.VMEM((n_xbc // LANES, tm + 2 * HALO, LANES), F32)],
        compiler_params=pltpu.CompilerParams(dimension_semantics=("parallel", "parallel"),
                                             vmem_limit_bytes=VMEM_LIMIT),
        name="inproj",
    )(x, x, x, meta_chunk, *weights)


def _spread_matrix(n):
    return np.kron(np.eye(n, dtype=np.float32), np.ones((1, SSM_HEADDIM), np.float32))


def _triangles():
    li = lax.broadcasted_iota(jnp.int32, (CHUNK, CHUNK), 0)
    si = lax.broadcasted_iota(jnp.int32, (CHUNK, CHUNK), 1)
    return (li >= si).astype(F32).astype(BF16), (li <= si).astype(F32).astype(BF16), li, si


def _hi_lo_rows(v):
    hi = v.astype(BF16).astype(F32)
    row = lax.broadcasted_iota(jnp.int32, (16, v.shape[1]), 0)
    return jnp.where(row == 0, hi, jnp.where(row == 1, v - hi, 0.0)).astype(BF16)


def _exp_lanes(row, lo, hi):
    lane = lax.broadcasted_iota(jnp.int32, row.shape, 1)
    return jnp.exp(jnp.where(jnp.logical_and(lane >= lo, lane < hi), row, 0.0))


def _ssd_bwd_state_kernel(x_ref, b_ref, decc_ref, cols_ref, spread_edge_ref, spread_tot_ref, sb_ref, state_ref,
                          *, n_heads, hpg):
    @pl.when(pl.program_id(1) == 0)
    def _():
        state_ref[...] = jnp.zeros_like(state_ref)

    gp = hpg * SSM_HEADDIM
    chunks = []
    for c in (1, 0):
        rows = slice(c * CHUNK, (c + 1) * CHUNK)
        wide = _dot(cols_ref[0, rows, :], spread_edge_ref[...])
        total = _exp_lanes(decc_ref[0, c * CHUNK:c * CHUNK + 1, :], n_heads, 2 * n_heads)
        chunks.append((c, rows, wide, _dot(_hi_lo_rows(total), spread_tot_ref[...])))
    for g in range(SSM_GROUPS):
        lanes = slice(g * gp, (g + 1) * gp)
        state = state_ref[g]
        for c, rows, wide, tot_wide in chunks:
            weighted = (x_ref[0, rows, lanes].astype(F32) * wide[:, lanes]).astype(BF16)
            s_local = lax.dot_general(b_ref[0, rows, g * D_STATE:(g + 1) * D_STATE], weighted,
                                      (((0,), (0,)), ((), ())), preferred_element_type=F32)
            sb_ref[0, c, g * D_STATE:(g + 1) * D_STATE, :] = state.astype(sb_ref.dtype)
            state = (tot_wide[0:1, lanes] + tot_wide[1:2, lanes]) * state + s_local
        state_ref[g] = state


def _ssd_main_kernel(x_ref, decc_ref, decr_ref, cols_ref, sb_ref, dskip_ref, spread3_ref,
                     y_ref, sf_ref, *, d_inner, n_heads, hpg):
    @pl.when(pl.program_id(1) == 0)
    def _():
        sf_ref[...] = jnp.zeros_like(sf_ref)

    gp = hpg * SSM_HEADDIM
    _, _, li, si = _triangles()
    fwd_sees = li > si
    bwd_sees = li < si
    left_head = lax.broadcasted_iota(jnp.int32, (CHUNK, 2 * SSM_HEADDIM), 1) < SSM_HEADDIM

    def b_of(g):
        return x_ref[0, :, d_inner + g * D_STATE:d_inner + (g + 1) * D_STATE]

    def c_of(g):
        return x_ref[0, :, d_inner + (SSM_GROUPS + g) * D_STATE:d_inner + (SSM_GROUPS + g + 1) * D_STATE]

    decc = decc_ref[0]
    decr = decr_ref[0]
    cbs = [lax.dot_general(c_of(g), b_of(g), (((1,), (1,)), ((), ())), preferred_element_type=F32)
           for g in range(SSM_GROUPS)]
    wide = _dot(cols_ref[0], spread3_ref[...])
    tot_wide = _dot(_hi_lo_rows(_exp_lanes(decc[CHUNK - 1:CHUNK, :], 0, n_heads)), spread3_ref[:, :d_inner])

    for g in range(SSM_GROUPS):
        lanes = slice(g * gp, (g + 1) * gp)
        wlanes = lambda q: slice(q * d_inner + g * gp, q * d_inner + (g + 1) * gp)
        xf = x_ref[0, :, lanes].astype(F32)
        s_in = sf_ref[g]
        y_off = _dot(c_of(g), jnp.concatenate(
            [s_in.astype(BF16), sb_ref[0, 0, g * D_STATE:(g + 1) * D_STATE, :]], axis=1))
        weighted = (xf * wide[:, wlanes(0)]).astype(BF16)
        s_local = lax.dot_general(b_of(g), weighted, (((0,), (0,)), ((), ())), preferred_element_type=F32)
        sf_ref[g] = (tot_wide[0:1, lanes] + tot_wide[1:2, lanes]) * s_in + s_local
        y = (y_off[:, :gp] * wide[:, wlanes(1)] + y_off[:, gp:] * wide[:, wlanes(2)]
             + xf * dskip_ref[:, lanes])

        pairs = []
        for j in range(hpg // 2):
            ms = []
            for h in (g * hpg + 2 * j, g * hpg + 2 * j + 1):
                hb, hd = n_heads + h, 2 * n_heads + h
                df = (jnp.broadcast_to(decc[:, h:h + 1], (CHUNK, CHUNK))
                      - jnp.broadcast_to(decr[h:h + 1, :], (CHUNK, CHUNK)))
                db = (jnp.broadcast_to(decc[:, hb:hb + 1], (CHUNK, CHUNK))
                      - jnp.broadcast_to(decr[hb:hb + 1, :], (CHUNK, CHUNK)))
                dd = jnp.broadcast_to(decr[hd:hd + 1, :], (CHUNK, CHUNK))
                ms.append((cbs[g] * jnp.exp(jnp.where(fwd_sees, df, jnp.where(bwd_sees, db, dd)))).astype(BF16))
            xp = x_ref[0, :, g * gp + 2 * j * SSM_HEADDIM:g * gp + (2 * j + 2) * SSM_HEADDIM]
            zero = jnp.zeros_like(xp)
            block_diag = jnp.concatenate([jnp.where(left_head, xp, zero), jnp.where(left_head, zero, xp)], axis=0)
            pairs.append(_dot(jnp.concatenate(ms, axis=1), block_diag))
        y_ref[0, :, lanes] = (y + jnp.concatenate(pairs, axis=1)).astype(y_ref.dtype)


def _ssd(xbc, decc, decr, cols, dskip, *, n_x_chunks, d_inner):
    bsz, frame, c = xbc.shape
    n_heads = decc.shape[2] // 4
    hpg = n_heads // SSM_GROUPS
    gp = hpg * SSM_HEADDIM
    gn = SSM_GROUPS * D_STATE
    nc = n_x_chunks + 1
    assert c == d_inner + 2 * gn and d_inner % gn == 0
    n_pairs = n_x_chunks // 2
    pair_block = lambda shape, col=0: pl.BlockSpec(
        (1,) + shape, lambda b, i: (b, n_pairs - 1 - i) + (col,) + (0,) * (len(shape) - 2))
    params = pltpu.CompilerParams(dimension_semantics=("parallel", "arbitrary"), vmem_limit_bytes=VMEM_LIMIT)
    state_scratch = pltpu.VMEM((SSM_GROUPS, D_STATE, gp), F32)
    spread1 = _spread_matrix(n_heads)
    blank = np.zeros_like(spread1)
    quantity = lambda q: np.concatenate([spread1 if r == q else blank for r in range(4)], axis=0)
    spread3 = jnp.asarray(np.concatenate([quantity(0), quantity(1), quantity(2)], axis=1), BF16)

    sb = pl.pallas_call(
        functools.partial(_ssd_bwd_state_kernel, n_heads=n_heads, hpg=hpg),
        grid=(bsz, n_pairs),
        in_specs=[pair_block((2 * CHUNK, d_inner)), pair_block((2 * CHUNK, gn), d_inner // gn),
                  pair_block((2 * CHUNK, 4 * n_heads)), pair_block((2 * CHUNK, 4 * n_heads)),
                  _resident((4 * n_heads, d_inner)), _resident((4 * n_heads, d_inner))],
        out_specs=pair_block((2, gn, gp)),
        out_shape=jax.ShapeDtypeStruct((bsz, n_x_chunks, gn, gp), BF16),
        scratch_shapes=[state_scratch],
        compiler_params=params,
        name="ssd_bwd_state",
    )(xbc, xbc, decc, cols, jnp.asarray(quantity(3), BF16), jnp.asarray(quantity(1), BF16))

    fwd = lambda i: (i + n_x_chunks) % nc
    row_block = lambda n: pl.BlockSpec((1, CHUNK, n), lambda b, i: (b, fwd(i), 0))
    state_block = pl.BlockSpec((1, 1, gn, gp), lambda b, i: (b, jnp.minimum(fwd(i), n_x_chunks - 1), 0, 0))
    return pl.pallas_call(
        functools.partial(_ssd_main_kernel, d_inner=d_inner, n_heads=n_heads, hpg=hpg),
        grid=(bsz, nc),
        in_specs=[row_block(c), row_block(4 * n_heads),
                  pl.BlockSpec((1, 3 * n_heads, CHUNK), lambda b, i: (b, 0, fwd(i))), row_block(4 * n_heads),
                  state_block, _resident(dskip.shape), _resident(spread3.shape)],
        out_specs=row_block(d_inner),
        out_shape=jax.ShapeDtypeStruct((bsz, frame, d_inner), BF16),
        scratch_shapes=[state_scratch],
        compiler_params=params,
        name="ssd_main",
    )(xbc, decc, decr, cols, sb, dskip, spread3)


def _tail_kernel(x_ref, y_ref, z_ref, gates_ref, a0_prev_ref, a0_ref, a0_next_ref, a_prev_ref, a_ref, a_next_ref,
                 cw_ref, cb_ref, lng_ref, lnb_ref, wco_ref, nw_ref, wso_ref, wo_ref, nffn_ref,
                 wg_ref, wu_ref, wd_ref, nfin_ref, o_ref, buf_ref, conv_ref, h_ref, act_ref, hn_ref):
    tm, d = x_ref.shape[1], x_ref.shape[2]
    d_inner = z_ref.shape[2]
    gc = d_inner // SSM_GROUPS
    n_tiles = buf_ref.shape[0]

    def fill_window(p_ref, m_ref, n_ref):
        for t in range(n_tiles):
            sl = slice(t * LANES, (t + 1) * LANES)
            buf_ref[t, 0:HALO, :] = p_ref[0, :, sl].astype(F32)
            buf_ref[t, HALO:HALO + tm, :] = m_ref[0, :, sl].astype(F32)
            buf_ref[t, HALO + tm:, :] = n_ref[0, :, sl].astype(F32)

    n_units = 2 * n_tiles

    def conv_unit(u):
        t, row0 = u // 2, (u % 2) * (tm // 2)
        acc = _dwconv_tile(buf_ref, cw_ref, cb_ref, CONF_KERNEL, t, tm // 2, row0)
        conv_ref[row0:row0 + tm // 2, t * LANES:(t + 1) * LANES] = acc
        return acc

    def norm_act():
        v = conv_ref[...]
        mu = jnp.mean(v, axis=-1, keepdims=True)
        vc = v - mu
        var = jnp.mean(vc * vc, axis=-1, keepdims=True)
        h = _silu(vc * lax.rsqrt(var + EPS) * lng_ref[...] + lnb_ref[...])
        h_ref[...] = h.astype(BF16)
        return h

    @pl.when(pl.program_id(1) == 0)
    def _():
        fill_window(a0_prev_ref, a0_ref, a0_next_ref)
        for u in range(n_units):
            conv_unit(u)
        norm_act()

    y_conv = _dot(h_ref[...], wco_ref[...])
    fill_window(a_prev_ref, a_ref, a_next_ref)

    v = y_ref[0].astype(F32) * _silu(z_ref[0].astype(F32))
    parts = []
    for g in range(SSM_GROUPS):
        vg = v[:, g * gc:(g + 1) * gc]
        ms = jnp.mean(vg * vg, axis=-1, keepdims=True)
        parts.append((vg * lax.rsqrt(ms + EPS) * nw_ref[:, g * gc:(g + 1) * gc]).astype(BF16))
    y_ssm = _dot(jnp.concatenate(parts, axis=1), wso_ref[...])

    merged = gates_ref[0, :, :d].astype(F32) * y_conv + gates_ref[0, :, d:].astype(F32) * y_ssm
    hs = x_ref[0] + _dot(merged.astype(BF16), wo_ref[...])

    ms = jnp.mean(hs * hs, axis=-1, keepdims=True)
    hn_ref[...] = (hs * lax.rsqrt(ms + EPS) * nffn_ref[...]).astype(BF16)
    n_ffn_chunks = wg_ref.shape[1] // MXU_COLS
    for c in range(n_ffn_chunks):
        sl = slice(c * MXU_COLS, (c + 1) * MXU_COLS)
        hn = hn_ref[...]
        act_ref[:, sl] = (_silu(_dot(hn, wg_ref[:, sl])) * _dot(hn, wu_ref[:, sl])).astype(BF16)
        for u in range(c * n_units // n_ffn_chunks, (c + 1) * n_units // n_ffn_chunks):
            _order_after(hn_ref if c + 1 < n_ffn_chunks else act_ref, conv_unit(u))
    hs = hs + _dot(act_ref[...], wd_ref[...])
    norm_act()

    ms = jnp.mean(hs * hs, axis=-1, keepdims=True)
    o_ref[0] = hs * lax.rsqrt(ms + EPS) * nfin_ref[...]


def _tail(x, y, z, gates, a, weights, *, tm):
    bsz, seq, d = x.shape
    d_inner = y.shape[2]
    conf = a.shape[2]
    d_ff = weights[-4].shape[1]
    per = tm // HALO
    last = seq // tm - 1
    meta_rows_block = (seq + META_PAD) // HALO
    blk = lambda n: pl.BlockSpec((1, tm, n), lambda b, j: (b, j, 0))
    halo = lambda index: pl.BlockSpec((1, HALO, conf), lambda b, j: (b, index(j), 0))
    nxt = lambda j: jnp.minimum(j + 1, last)
    return pl.pallas_call(
        _tail_kernel,
        grid=(bsz, seq // tm),
        in_specs=[blk(d), blk(d_inner), blk(d_inner), blk(2 * d),
                  halo(lambda j: meta_rows_block), pl.BlockSpec((1, tm, conf), lambda b, j: (b, 0, 0)),
                  halo(lambda j: per),
                  halo(lambda j: (j + 1) * per - 1), pl.BlockSpec((1, tm, conf), lambda b, j: (b, nxt(j), 0)),
                  halo(lambda j: (nxt(j) + 1) * per)]
                 + [_resident(w.shape) for w in weights],
        out_specs=blk(d),
        out_shape=jax.ShapeDtypeStruct((bsz, seq, d), F32),
        scratch_shapes=[pltpu.VMEM((conf // LANES, tm + 2 * HALO, LANES), F32), pltpu.VMEM((tm, conf), F32),
                        pltpu.VMEM((tm, conf), BF16), pltpu.VMEM((tm, d_ff), BF16), pltpu.VMEM((tm, d), BF16)],
        compiler_params=pltpu.CompilerParams(dimension_semantics=("parallel", "arbitrary"),
                                             vmem_limit_bytes=VMEM_LIMIT),
        name="tail",
    )(x, y, z, gates, a, a, a, a, a, a, *weights)


def kernel(x, meta_tokens, norm_mix, w_in, conv_dw_w, conv_dw_b, conv_ln_g, conv_ln_b, conv_out_w,
           ssm_conv_w, ssm_conv_b, dt_bias_f, dt_bias_b, a_log_f, a_log_b, ssm_d, ssm_norm_w, ssm_out_w,
           w_o, norm_ffn, w_gate, w_up, w_down, norm_final):
    bsz, seq, d = x.shape
    assert norm_mix.shape[0] == 1 and seq % ROW_TILE == 0 and meta_tokens.shape[0] == N_META
    conf = conv_dw_w.shape[2]
    d_inner = ssm_norm_w.shape[1]
    n_heads = ssm_d.shape[1]
    n_xbc = ssm_conv_w.shape[2]
    assert conf == d

    w = w_in[0].astype(BF16)
    o_dt = 2 * conf + d_inner + n_xbc
    o_g = o_dt + 2 * n_heads
    dt_bias = jnp.concatenate([dt_bias_f[0], dt_bias_b[0]])
    a_log = jnp.concatenate([a_log_f[0], a_log_b[0]])
    meta_chunk = jnp.concatenate([jnp.zeros((META_PAD, d), x.dtype), meta_tokens.astype(x.dtype)], axis=0)

    a, z, xbc, decc, decr, cols, gates = _inproj(
        x, meta_chunk, norm_mix, w, w[:, o_g:], dt_bias[None, :], dt_bias[:, None], a_log[None, :], a_log[:, None],
        ssm_conv_w[0], ssm_conv_b, tm=ROW_TILE, conf=conf, d_inner=d_inner)
    y = _ssd(xbc, decc, decr, cols, jnp.repeat(ssm_d[0], SSM_HEADDIM)[None, :],
             n_x_chunks=seq // CHUNK, d_inner=d_inner)
    tail_weights = [conv_dw_w[0], conv_dw_b, conv_ln_g, conv_ln_b, conv_out_w[0].astype(BF16), ssm_norm_w,
                    ssm_out_w[0].astype(BF16), w_o[0].astype(BF16), norm_ffn, w_gate[0].astype(BF16),
                    w_up[0].astype(BF16), w_down[0].astype(BF16), norm_final[None, :]]
    return _tail(x, y, z, gates, a, tail_weights, tm=ROW_TILE)
```
